```python
import math
import jax, jax.numpy as jnp
from jax import lax
import numpy as np

D_MODEL = 1024
BATCH = 32
SEQ = 256
DEPTH = 4
DEC_BATCH = 4
DEC_SEQ = 1024
PAST_LEN = 256

GRID_W = 64
LRU_WIDTH = 1024
LRU_HEADS = 8
LRU_BLOCK = LRU_WIDTH // LRU_HEADS
LRU_CONV = 4
LRU_C = 8.0
CONF_WIDTH = 1024
CONF_KERNEL = 31
CHUNK = 128
SGU_WIDTH = 2048
SGU_HEADS = 8
SGU_HEAD_DIM = SGU_WIDTH // SGU_HEADS
D_FF = 2816
FFN_CONV = 3
N_AB = (DEPTH + 1) // 2
N_C = DEPTH // 2
N_MOD = 6
EPS = 1e-6
POS_BASE = 10000.0

kernel_name = "hybrid_rglru_conformer_sgu_diffusion_step"


def rms_norm(x, g):
    xf = x.astype(jnp.float32)
    y = xf * lax.rsqrt(jnp.mean(xf * xf, axis=-1, keepdims=True) + EPS)
    return (y * g.astype(jnp.float32)).astype(x.dtype)


def layer_norm(x, g, b):
    xf = x.astype(jnp.float32)
    mu = jnp.mean(xf, axis=-1, keepdims=True)
    var = jnp.mean(jnp.square(xf - mu), axis=-1, keepdims=True)
    y = (xf - mu) * lax.rsqrt(var + EPS)
    return (y * g.astype(jnp.float32) + b.astype(jnp.float32)).astype(x.dtype)


def dw_conv(x, w, b, pad_left, pad_right):
    out = lax.conv_general_dilated(
        x, w[:, None, :].astype(x.dtype), window_strides=(1,),
        padding=[(pad_left, pad_right)], dimension_numbers=('NWC', 'WIO', 'NWC'),
        feature_group_count=x.shape[-1])
    return out + b.astype(x.dtype)


def grid_pos_embedding(t, d, dtype):
    rows = t // GRID_W
    row = jnp.repeat(jnp.arange(rows), GRID_W).astype(jnp.float32)
    col = jnp.tile(jnp.arange(GRID_W), rows).astype(jnp.float32)
    quarter = d // 4
    freq = jnp.exp(-math.log(POS_BASE) * jnp.arange(quarter, dtype=jnp.float32) / quarter)

    def enc(p):
        ang = p[:, None] * freq[None, :]
        return jnp.concatenate([jnp.sin(ang), jnp.cos(ang)], axis=-1)

    return jnp.concatenate([enc(row), enc(col)], axis=-1).astype(dtype)


def linear_scan(a, b, reverse):
    def combine(p, q):
        a1, b1 = p
        a2, b2 = q
        return a1 * a2, a2 * b1 + b2
    return lax.associative_scan(combine, (a, b), reverse=reverse, axis=1)


def mixer_ab(xin, w_in, lru_conv_w, lru_conv_b, lru_w_gates, lru_b_gates, lru_lambda,
             conf_conv_w, conf_conv_b, conf_ln_g, conf_ln_b, w_out, h0):
    z = xin @ w_in
    gate_br, x_br, glu_a, glu_b = jnp.split(
        z, [LRU_WIDTH, 2 * LRU_WIDTH, 2 * LRU_WIDTH + CONF_WIDTH], axis=-1)
    xc = dw_conv(x_br, lru_conv_w, lru_conv_b, LRU_CONV // 2, LRU_CONV - 1 - LRU_CONV // 2)
    bsz, t, _ = xc.shape
    xh = xc.reshape(bsz, t, LRU_HEADS, LRU_BLOCK)
    gl = jnp.einsum('bthi,dghij->dgbthj', xh, lru_w_gates).reshape(2, 2, bsz, t, LRU_WIDTH)
    gates = jax.nn.sigmoid((gl + lru_b_gates[:, :, None, None, :]).astype(jnp.float32))
    r, i = gates[:, 0], gates[:, 1]
    log_a = -LRU_C * r * jax.nn.softplus(-lru_lambda.astype(jnp.float32))[:, None, None, :]
    a = jnp.exp(log_a)
    bt = jnp.sqrt(-jnp.expm1(2.0 * log_a)) * i * xc.astype(jnp.float32)[None]
    dec_f, hf = linear_scan(a[0], bt[0], False)
    dec_b, hb = linear_scan(a[1], bt[1], True)
    if h0 is not None:
        hf = hf + dec_f * h0[:, 0, None, :].astype(jnp.float32)
        hb = hb + dec_b * h0[:, 1, None, :].astype(jnp.float32)
    y_a = jax.nn.gelu(gate_br) * (hf + hb).astype(xin.dtype)
    g = glu_a * jax.nn.sigmoid(glu_b)
    g = dw_conv(g, conf_conv_w, conf_conv_b, CONF_KERNEL // 2, CONF_KERNEL // 2)
    y_b = jax.nn.silu(layer_norm(g, conf_ln_g, conf_ln_b))
    out = jnp.concatenate([y_a, y_b], axis=-1) @ w_out
    return out, hf[:, -1], hb[:, 0]


def mixer_c(xin, w_in, ln_g, ln_b, w_s, b_s, w_out):
    z = jax.nn.gelu(xin @ w_in)
    u, v = jnp.split(z, 2, axis=-1)
    v = layer_norm(v, ln_g, ln_b)
    bsz, t, _ = v.shape
    vc = v.reshape(bsz, t // CHUNK, CHUNK, SGU_HEADS, SGU_HEAD_DIM)
    s = jnp.einsum('bnphd,hqp->bnqhd', vc, w_s) + b_s.T[None, None, :, :, None]
    return (u * s.reshape(bsz, t, SGU_WIDTH)) @ w_out


def conv_ffn(xin, w_up, conv_w, conv_b, w_down):
    z = dw_conv(xin @ w_up, conv_w, conv_b, FFN_CONV // 2, FFN_CONV // 2)
    gate, val = jnp.split(z, 2, axis=-1)
    return (jax.nn.gelu(gate) * val) @ w_down


def trunk(x, cond, h0_all, mod_w, mod_b, norm_g,
          ab_w_in, lru_conv_w, lru_conv_b, lru_w_gates, lru_b_gates, lru_lambda,
          conf_conv_w, conf_conv_b, conf_ln_g, conf_ln_b, ab_w_out,
          c_w_in, c_ln_g, c_ln_b, c_w_s, c_b_s, c_w_out,
          ffn_w_up, ffn_conv_w, ffn_conv_b, ffn_w_down):
    sc = jax.nn.silu(cond)
    states = []
    for l in range(DEPTH):
        m = (sc @ mod_w[l] + mod_b[l])[:, None, :]
        sh1, sc1, g1, sh2, sc2, g2 = jnp.split(m, N_MOD, axis=-1)
        h = rms_norm(x, norm_g[l, 0]) * (1.0 + sc1) + sh1
        j = l // 2
        if l % 2 == 0:
            h0 = None if h0_all is None else h0_all[:, j]
            h, hf_last, hb_first = mixer_ab(
                h, ab_w_in[j], lru_conv_w[j], lru_conv_b[j], lru_w_gates[j], lru_b_gates[j],
                lru_lambda[j], conf_conv_w[j], conf_conv_b[j], conf_ln_g[j], conf_ln_b[j],
                ab_w_out[j], h0)
            states.append(jnp.stack([hf_last, hb_first], axis=1))
        else:
            h = mixer_c(h, c_w_in[j], c_ln_g[j], c_ln_b[j], c_w_s[j], c_b_s[j], c_w_out[j])
        x = x + g1 * rms_norm(h, norm_g[l, 1])
        h = rms_norm(x, norm_g[l, 2]) * (1.0 + sc2) + sh2
        h = conv_ffn(h, ffn_w_up[l], ffn_conv_w[l], ffn_conv_b[l], ffn_w_down[l])
        x = x + g2 * rms_norm(h, norm_g[l, 3])
    return x, jnp.stack(states, axis=1)


def setup_inputs(seed: int = 0) -> dict:
    key = jax.random.key(seed)
    ks = iter(jax.random.split(key, 40))

    def nrm(shape, scale):
        return jax.random.normal(next(ks), shape, jnp.float32) * scale

    D = D_MODEL
    a0 = jax.random.uniform(next(ks), (N_AB, 2, LRU_WIDTH), jnp.float32, 0.9, 0.999)
    sig = a0 ** (1.0 / LRU_C)
    lru_lambda = jnp.log(sig) - jnp.log1p(-sig)
    return {
        "x_prompt": nrm((BATCH, SEQ, D), 1.0),
        "x_sample": nrm((DEC_BATCH, DEC_SEQ, D), 1.0),
        "state_lru": nrm((DEC_BATCH, N_AB, 2, LRU_WIDTH), 0.5),
        "c": nrm((DEC_BATCH, D), 1.0),
        "c_ctx": nrm((D,), 1.0),
        "mod_w": nrm((DEPTH, D, N_MOD * D), 0.5 * D ** -0.5),
        "mod_b": nrm((DEPTH, N_MOD * D), 0.02),
        "norm_g": 1.0 + nrm((DEPTH, 4, D), 0.02),
        "ab_w_in": nrm((N_AB, D, 2 * LRU_WIDTH + 2 * CONF_WIDTH), D ** -0.5),
        "lru_conv_w": nrm((N_AB, LRU_CONV, LRU_WIDTH), LRU_CONV ** -0.5),
        "lru_conv_b": nrm((N_AB, LRU_WIDTH), 0.02),
        "lru_w_gates": nrm((N_AB, 2, 2, LRU_HEADS, LRU_BLOCK, LRU_BLOCK), LRU_BLOCK ** -0.5),
        "lru_b_gates": nrm((N_AB, 2, 2, LRU_WIDTH), 0.02),
        "lru_lambda": lru_lambda,
        "conf_conv_w": nrm((N_AB, CONF_KERNEL, CONF_WIDTH), CONF_KERNEL ** -0.5),
        "conf_conv_b": nrm((N_AB, CONF_WIDTH), 0.02),
        "conf_ln_g": 1.0 + nrm((N_AB, CONF_WIDTH), 0.02),
        "conf_ln_b": nrm((N_AB, CONF_WIDTH), 0.02),
        "ab_w_out": nrm((N_AB, LRU_WIDTH + CONF_WIDTH, D), (LRU_WIDTH + CONF_WIDTH) ** -0.5),
        "c_w_in": nrm((N_C, D, 2 * SGU_WIDTH), D ** -0.5),
        "c_ln_g": 1.0 + nrm((N_C, SGU_WIDTH), 0.02),
        "c_ln_b": nrm((N_C, SGU_WIDTH), 0.02),
        "c_w_s": nrm((N_C, SGU_HEADS, CHUNK, CHUNK), CHUNK ** -0.5),
        "c_b_s": 1.0 + nrm((N_C, SGU_HEADS, CHUNK), 0.02),
        "c_w_out": nrm((N_C, SGU_WIDTH, D), SGU_WIDTH ** -0.5),
        "ffn_w_up": nrm((DEPTH, D, 2 * D_FF), D ** -0.5),
        "ffn_conv_w": nrm((DEPTH, FFN_CONV, 2 * D_FF), FFN_CONV ** -0.5),
        "ffn_conv_b": nrm((DEPTH, 2 * D_FF), 0.02),
        "ffn_w_down": nrm((DEPTH, D_FF, D), D_FF ** -0.5),
    }


def reference(x_prompt, x_sample, state_lru, c, c_ctx, mod_w, mod_b, norm_g,
              ab_w_in, lru_conv_w, lru_conv_b, lru_w_gates, lru_b_gates, lru_lambda,
              conf_conv_w, conf_conv_b, conf_ln_g, conf_ln_b, ab_w_out,
              c_w_in, c_ln_g, c_ln_b, c_w_s, c_b_s, c_w_out,
              ffn_w_up, ffn_conv_w, ffn_conv_b, ffn_w_down):
    y_prompt, new_state_lru = trunk(
        x_prompt, c_ctx[None, :], None, mod_w, mod_b, norm_g,
        ab_w_in, lru_conv_w, lru_conv_b, lru_w_gates, lru_b_gates, lru_lambda,
        conf_conv_w, conf_conv_b, conf_ln_g, conf_ln_b, ab_w_out,
        c_w_in, c_ln_g, c_ln_b, c_w_s, c_b_s, c_w_out,
        ffn_w_up, ffn_conv_w, ffn_conv_b, ffn_w_down)
    t = x_sample.shape[1]
    xs = x_sample + grid_pos_embedding(t, x_sample.shape[-1], x_sample.dtype)[None]
    y_sample, _ = trunk(
        xs, c, state_lru, mod_w, mod_b, norm_g,
        ab_w_in, lru_conv_w, lru_conv_b, lru_w_gates, lru_b_gates, lru_lambda,
        conf_conv_w, conf_conv_b, conf_ln_g, conf_ln_b, ab_w_out,
        c_w_in, c_ln_g, c_ln_b, c_w_s, c_b_s, c_w_out,
        ffn_w_up, ffn_conv_w, ffn_conv_b, ffn_w_down)
    return (y_prompt, y_sample, new_state_lru)
```

```python
import functools
import math

import jax
import jax.numpy as jnp
from jax import lax
from jax.experimental import pallas as pl
from jax.experimental.pallas import tpu as pltpu

D_MODEL = 1024
DEPTH = 4
GRID_W = 64
LRU_WIDTH = 1024
LRU_HEADS = 8
LRU_BLOCK = LRU_WIDTH // LRU_HEADS
LRU_CONV = 4
LRU_C = 8.0
CONF_WIDTH = 1024
CONF_KERNEL = 31
CHUNK = 128
SGU_WIDTH = 2048
SGU_HEADS = 8
SGU_HEAD_DIM = SGU_WIDTH // SGU_HEADS
D_FF = 2816
FFN_CONV = 3
N_MOD = 6
EPS = 1e-6
POS_BASE = 10000.0

SUBLANES = 8
LANES = 128
COL_GROUP = 256
MOD_ROWS = 8
LRU_PAD = 8
CONF_PAD = 16
CONF_ROWS = 32
VMEM_LIMIT = 56 * 1024 * 1024

_BF16 = jnp.bfloat16
_F32 = jnp.float32


def _rms(x, g):
    return x * lax.rsqrt(jnp.mean(x * x, axis=-1, keepdims=True) + EPS) * g


def _dot(a, b):
    return jnp.dot(a, b, preferred_element_type=_F32)


def _const_spec(shape):
    zeros = (0,) * len(shape)
    return pl.BlockSpec(shape, lambda *_: zeros, pipeline_mode=pl.Buffered(1))


def _mod_kernel(cond_ref, w_ref, b_ref, o_ref):
    s = jax.nn.silu(cond_ref[...]).astype(_BF16)
    o_ref[0, 0] = _dot(s, w_ref[0].astype(_BF16)) + b_ref[0, 0]


def _modulation(cond8, mod_w, mod_b):
    d = D_MODEL
    return pl.pallas_call(
        _mod_kernel,
        grid=(DEPTH, N_MOD),
        in_specs=[
            pl.BlockSpec((MOD_ROWS, d), lambda l, n: (0, 0)),
            pl.BlockSpec((1, d, d), lambda l, n: (l, 0, n)),
            pl.BlockSpec((1, 1, 1, d), lambda l, n: (l, n, 0, 0)),
        ],
        out_specs=pl.BlockSpec((1, 1, MOD_ROWS, d), lambda l, n: (l, n, 0, 0)),
        out_shape=jax.ShapeDtypeStruct((DEPTH, N_MOD, MOD_ROWS, d), _F32),
        compiler_params=pltpu.CompilerParams(
            dimension_semantics=("arbitrary", "arbitrary"), vmem_limit_bytes=VMEM_LIMIT),
        name="modulation",
    )(cond8, mod_w, mod_b.reshape(DEPTH, N_MOD, 1, d))


def _ab_kernel(*refs, m, t, has_pos, has_h0, emit_state):
    nseq = m // t
    nlane = COL_GROUP // LANES
    lc = t // SUBLANES
    refs = list(refs)
    x_ref = refs.pop(0)
    pos_ref = refs.pop(0) if has_pos else None
    mod_ref = refs.pop(0)
    ng_ref = refs.pop(0)
    h0_ref = refs.pop(0) if has_h0 else None
    (w_in_ref, cw_ref, cb_ref, wg_ref, bg_ref, lam_ref,
     ccw_ref, ccb_ref, lng_ref, lnb_ref, w_out_ref) = refs[:11]
    refs = refs[11:]
    xo_ref = refs.pop(0)
    st_ref = refs.pop(0) if emit_state else None
    (h_scr, ycat_scr, gconv_scr, af_scr, bf_scr, ab_scr, bb_scr, xpad_scr, gpad_scr) = refs

    if has_pos:
        xo_ref[...] = x_ref[...] + pos_ref[...]
    else:
        xo_ref[...] = x_ref[...]
    mod = mod_ref[0]
    sh1, sc1, g1 = mod[0:1], mod[1:2], mod[2:3]
    h_scr[...] = (_rms(xo_ref[...], ng_ref[0:1]) * (1.0 + sc1) + sh1).astype(_BF16)

    zero_lru = jnp.zeros((nseq, LRU_PAD, COL_GROUP), _F32)
    xpad_scr[:, 0:LRU_PAD, :] = zero_lru
    xpad_scr[:, LRU_PAD + t:, :] = zero_lru
    zero_conf = jnp.zeros((nseq, CONF_PAD, COL_GROUP), _F32)
    gpad_scr[:, 0:CONF_PAD, :] = zero_conf
    gpad_scr[:, CONF_PAD + t:, :] = zero_conf

    neg_lam = -lam_ref[...]
    sp = jnp.maximum(neg_lam, 0.0) + jnp.log1p(jnp.exp(-jnp.abs(neg_lam)))

    for grp in range(LRU_WIDTH // COL_GROUP):
        c0 = grp * COL_GROUP
        cols = slice(c0, c0 + COL_GROUP)
        h = h_scr[...]

        zx = _dot(h, w_in_ref[:, LRU_WIDTH + c0:LRU_WIDTH + c0 + COL_GROUP])
        for q in range(nseq):
            xpad_scr[q, LRU_PAD:LRU_PAD + t, :] = zx[q * t:(q + 1) * t]
        left = LRU_CONV // 2
        xc_parts = []
        for q in range(nseq):
            acc = jnp.broadcast_to(cb_ref[:, cols], (t, COL_GROUP))
            for k in range(LRU_CONV):
                off = LRU_PAD + k - left
                acc = acc + cw_ref[k:k + 1, cols] * xpad_scr[q, off:off + t, :]
            xc_parts.append(acc)
        xc = xc_parts[0] if nseq == 1 else jnp.concatenate(xc_parts, axis=0)

        xcb = xc.astype(_BF16)
        heads_per_group = COL_GROUP // LRU_BLOCK
        gl_parts = []
        for hh in range(heads_per_group):
            head = grp * heads_per_group + hh
            gl_parts.append(_dot(xcb[:, hh * LRU_BLOCK:(hh + 1) * LRU_BLOCK], wg_ref[head]))

        def gate(dg):
            parts = [gp[:, dg * LRU_BLOCK:(dg + 1) * LRU_BLOCK] for gp in gl_parts]
            return jax.nn.sigmoid(jnp.concatenate(parts, axis=1) + bg_ref[dg:dg + 1, cols])

        for d, (a_scr, b_scr) in enumerate(((af_scr, bf_scr), (ab_scr, bb_scr))):
            r = gate(2 * d)
            i_gate = gate(2 * d + 1)
            a = jnp.exp((-LRU_C) * r * sp[d:d + 1, cols])
            b = jnp.sqrt(1.0 - a * a) * i_gate * xc
            for v in range(nlane):
                a_scr[v] = a[:, v * LANES:(v + 1) * LANES]
                b_scr[v] = b[:, v * LANES:(v + 1) * LANES]

        def scan_body(i, carry):
            out = []
            for q in range(nseq):
                for v in range(nlane):
                    hf, df, hb, db = carry[q * nlane + v]
                    rows_f = pl.ds(q * t + i, SUBLANES, stride=lc)
                    rows_b = pl.ds(q * t + (lc - 1 - i), SUBLANES, stride=lc)
                    a = af_scr[v, rows_f, :]
                    hf = a * hf + bf_scr[v, rows_f, :]
                    df = df * a
                    bf_scr[v, rows_f, :] = hf
                    af_scr[v, rows_f, :] = df
                    a = ab_scr[v, rows_b, :]
                    hb = a * hb + bb_scr[v, rows_b, :]
                    db = db * a
                    bb_scr[v, rows_b, :] = hb
                    ab_scr[v, rows_b, :] = db
                    out.append((hf, df, hb, db))
            return tuple(out)

        zeros8 = jnp.zeros((SUBLANES, LANES), _F32)
        ones8 = jnp.ones((SUBLANES, LANES), _F32)
        ends = lax.fori_loop(0, lc, scan_body,
                             tuple((zeros8, ones8, zeros8, ones8) for _ in range(nseq * nlane)))

        zg = _dot(h, w_in_ref[:, cols])
        gz = jax.nn.gelu(zg)
        for q in range(nseq):
            for v in range(nlane):
                hf_end, df_end, hb_end, db_end = ends[q * nlane + v]
                vcols = slice(c0 + v * LANES, c0 + (v + 1) * LANES)
                if has_h0:
                    cf = h0_ref[q, 0:1, vcols]
                    cb = h0_ref[q, 1:2, vcols]
                else:
                    cf = jnp.zeros((1, LANES), _F32)
                    cb = cf
                cfs = []
                for s in range(SUBLANES):
                    cfs.append(cf)
                    cf = hf_end[s:s + 1] + df_end[s:s + 1] * cf
                cbs = [None] * SUBLANES
                for s in reversed(range(SUBLANES)):
                    cbs[s] = cb
                    cb = hb_end[s:s + 1] + db_end[s:s + 1] * cb
                if emit_state:
                    st_ref[q, 0:1, vcols] = cf
                    st_ref[q, 1:2, vcols] = cb
                for s in range(SUBLANES):
                    rows = slice(q * t + s * lc, q * t + (s + 1) * lc)
                    hsum = (bf_scr[v, rows, :] + af_scr[v, rows, :] * cfs[s]
                            + bb_scr[v, rows, :] + ab_scr[v, rows, :] * cbs[s])
                    ycat_scr[rows, vcols] = (
                        gz[rows, v * LANES:(v + 1) * LANES] * hsum).astype(_BF16)

        za = _dot(h, w_in_ref[:, 2 * LRU_WIDTH + c0:2 * LRU_WIDTH + c0 + COL_GROUP])
        zb = _dot(h, w_in_ref[:, 2 * LRU_WIDTH + CONF_WIDTH + c0:
                              2 * LRU_WIDTH + CONF_WIDTH + c0 + COL_GROUP])
        glu = za * jax.nn.sigmoid(zb)
        for q in range(nseq):
            gpad_scr[q, CONF_PAD:CONF_PAD + t, :] = glu[q * t:(q + 1) * t]

        def conv_body(i, _):
            r0 = pl.multiple_of(i * CONF_ROWS, CONF_ROWS)
            win_rows = CONF_ROWS + 2 * CONF_PAD
            for q in range(nseq):
                win = gpad_scr[q, pl.ds(r0, win_rows), :]
                acc = jnp.broadcast_to(ccb_ref[:, cols], (CONF_ROWS, COL_GROUP))
                for phase in range(SUBLANES):
                    shifted = win if phase == 0 else pltpu.roll(win, win_rows - phase, 0)
                    for k in range(CONF_KERNEL):
                        off = CONF_PAD - CONF_KERNEL // 2 + k
                        if off % SUBLANES == phase:
                            base = off - phase
                            acc = acc + ccw_ref[k:k + 1, cols] * shifted[base:base + CONF_ROWS]
                gconv_scr[pl.ds(q * t + r0, CONF_ROWS), cols] = acc
            return 0

        lax.fori_loop(0, t // CONF_ROWS, conv_body, 0)

    gc = gconv_scr[...]
    mu = jnp.mean(gc, axis=-1, keepdims=True)
    gcc = gc - mu
    var = jnp.mean(gcc * gcc, axis=-1, keepdims=True)
    yb = gcc * lax.rsqrt(var + EPS) * lng_ref[...] + lnb_ref[...]
    ycat_scr[:, LRU_WIDTH:] = jax.nn.silu(yb).astype(_BF16)

    o = _dot(ycat_scr[...], w_out_ref[...])
    xo_ref[...] = xo_ref[...] + g1 * _rms(o, ng_ref[1:2])


def _mixer_ab(x2, pos, mod, ng, h0, w_in, cw, cb, wg, bg, lam, ccw, ccb, lng, lnb, w_out,
              *, m, t, mod_per_block, emit_state):
    rows, d = x2.shape
    nblk = rows // m
    nseq = m // t
    has_pos = pos is not None
    has_h0 = h0 is not None
    args = [x2]
    in_specs = [pl.BlockSpec((m, d), lambda i: (i, 0), pipeline_mode=pl.Buffered(1))]
    if has_pos:
        args.append(pos)
        in_specs.append(_const_spec(pos.shape))
    args.append(mod)
    in_specs.append(pl.BlockSpec((1, N_MOD, d), (lambda i: (i, 0, 0)) if mod_per_block
                                 else (lambda i: (0, 0, 0))))
    args.append(ng)
    in_specs.append(_const_spec(ng.shape))
    if has_h0:
        args.append(h0)
        in_specs.append(pl.BlockSpec((nseq, 2, LRU_WIDTH), lambda i: (i, 0, 0)))
    for w in (w_in, cw, cb, wg, bg, lam, ccw, ccb, lng, lnb, w_out):
        args.append(w)
        in_specs.append(_const_spec(w.shape))
    out_shape = [jax.ShapeDtypeStruct((rows, d), _F32)]
    out_specs = [pl.BlockSpec((m, d), lambda i: (i, 0))]
    if emit_state:
        out_shape.append(jax.ShapeDtypeStruct((rows // t, 2, LRU_WIDTH), _F32))
        out_specs.append(pl.BlockSpec((nseq, 2, LRU_WIDTH), lambda i: (i, 0, 0)))
    scratch = [
        pltpu.VMEM((m, d), _BF16),
        pltpu.VMEM((m, LRU_WIDTH + CONF_WIDTH), _BF16),
        pltpu.VMEM((m, CONF_WIDTH), _F32),
        pltpu.VMEM((COL_GROUP // LANES, m, LANES), _F32),
        pltpu.VMEM((COL_GROUP // LANES, m, LANES), _F32),
        pltpu.VMEM((COL_GROUP // LANES, m, LANES), _F32),
        pltpu.VMEM((COL_GROUP // LANES, m, LANES), _F32),
        pltpu.VMEM((nseq, t + 2 * LRU_PAD, COL_GROUP), _F32),
        pltpu.VMEM((nseq, t + 2 * CONF_PAD, COL_GROUP), _F32),
    ]
    res = pl.pallas_call(
        functools.partial(_ab_kernel, m=m, t=t, has_pos=has_pos, has_h0=has_h0,
                          emit_state=emit_state),
        grid=(nblk,),
        in_specs=in_specs,
        out_specs=out_specs,
        out_shape=out_shape,
        scratch_shapes=scratch,
        compiler_params=pltpu.CompilerParams(
            dimension_semantics=("arbitrary",), vmem_limit_bytes=VMEM_LIMIT),
        name="mixer_ab",
    )(*args)
    return res if emit_state else (res[0], None)


def _c_kernel(x_ref, mod_ref, ng_ref, w_in_ref, lng_ref, lnb_ref, ws_ref, bs_ref, w_out_ref,
              xo_ref, h_scr, v_scr, y_scr, *, m):
    mod = mod_ref[0]
    sh1, sc1, g1 = mod[0:1], mod[1:2], mod[2:3]
    h_scr[...] = (_rms(x_ref[...], ng_ref[0:1]) * (1.0 + sc1) + sh1).astype(_BF16)
    h = h_scr[...]

    for hd in range(SGU_HEADS):
        c0 = SGU_WIDTH + hd * SGU_HEAD_DIM
        v_scr[:, hd * SGU_HEAD_DIM:(hd + 1) * SGU_HEAD_DIM] = jax.nn.gelu(
            _dot(h, w_in_ref[:, c0:c0 + SGU_HEAD_DIM]))
    v = v_scr[...]
    mu = jnp.mean(v, axis=-1, keepdims=True)
    vc = v - mu
    var = jnp.mean(vc * vc, axis=-1, keepdims=True)
    rstd = lax.rsqrt(var + EPS)

    for hd in range(SGU_HEADS):
        cols = slice(hd * SGU_HEAD_DIM, (hd + 1) * SGU_HEAD_DIM)
        vn = ((v_scr[:, cols] - mu) * rstd * lng_ref[:, cols] + lnb_ref[:, cols]).astype(_BF16)
        u = jax.nn.gelu(_dot(h, w_in_ref[:, cols]))
        w_s = ws_ref[hd]
        bias = bs_ref[:, hd:hd + 1]
        for n in range(m // CHUNK):
            rows = slice(n * CHUNK, (n + 1) * CHUNK)
            s = _dot(w_s, vn[rows]) + bias
            y_scr[rows, cols] = (u[rows] * s).astype(_BF16)

    o = _dot(y_scr[...], w_out_ref[...])
    xo_ref[...] = x_ref[...] + g1 * _rms(o, ng_ref[1:2])


def _mixer_c(x2, mod, ng, w_in, lng, lnb, ws, bs_t, w_out, *, m, blocks_per_mod):
    rows, d = x2.shape
    in_specs = [
        pl.BlockSpec((m, d), lambda i: (i, 0)),
        pl.BlockSpec((1, N_MOD, d), lambda i: (i // blocks_per_mod, 0, 0)),
    ] + [_const_spec(w.shape) for w in (ng, w_in, lng, lnb, ws, bs_t, w_out)]
    return pl.pallas_call(
        functools.partial(_c_kernel, m=m),
        grid=(rows // m,),
        in_specs=in_specs,
        out_specs=pl.BlockSpec((m, d), lambda i: (i, 0)),
        out_shape=jax.ShapeDtypeStruct((rows, d), _F32),
        scratch_shapes=[
            pltpu.VMEM((m, d), _BF16),
            pltpu.VMEM((m, SGU_WIDTH), _F32),
            pltpu.VMEM((m, SGU_WIDTH), _BF16),
        ],
        compiler_params=pltpu.CompilerParams(
            dimension_semantics=("arbitrary",), vmem_limit_bytes=VMEM_LIMIT),
        name="mixer_c",
    )(x2, mod, ng, w_in, lng, lnb, ws, bs_t, w_out)


def _ffn_kernel(x_ref, mod_ref, ng_ref, wg_ref, wv_ref, cwg_ref, cwv_ref, cbg_ref, cbv_ref,
                wd_ref, xo_ref, h_scr, acc_scr, *, m, t, nj):
    j = pl.program_id(1)

    @pl.when(j == 0)
    def _():
        mod = mod_ref[0]
        sh2, sc2 = mod[3:4], mod[4:5]
        h_scr[...] = (_rms(x_ref[...], ng_ref[2:3]) * (1.0 + sc2) + sh2).astype(_BF16)
        acc_scr[...] = jnp.zeros_like(acc_scr)

    h = h_scr[...]
    tn = wg_ref.shape[1]
    pos_in_seq = lax.broadcasted_iota(jnp.int32, (m, tn), 0) & (t - 1)
    first = pos_in_seq == 0
    last = pos_in_seq == t - 1

    def conv(z, cw_ref, cb_ref):
        prev = jnp.where(first, 0.0, pltpu.roll(z, 1, 0))
        nxt = jnp.where(last, 0.0, pltpu.roll(z, m - 1, 0))
        return cw_ref[0:1] * prev + cw_ref[1:2] * z + cw_ref[2:3] * nxt + cb_ref[...]

    gate = conv(_dot(h, wg_ref[...]), cwg_ref, cbg_ref)
    val = conv(_dot(h, wv_ref[...]), cwv_ref, cbv_ref)
    act = (jax.nn.gelu(gate) * val).astype(_BF16)
    acc_scr[...] += _dot(act, wd_ref[...])

    @pl.when(j == nj - 1)
    def _():
        g2 = mod_ref[0][5:6]
        xo_ref[...] = x_ref[...] + g2 * _rms(acc_scr[...], ng_ref[3:4])


def _conv_ffn(x2, mod, ng, w_up, conv_w, conv_b, w_down, *, m, t, blocks_per_mod):
    rows, d = x2.shape
    tn = COL_GROUP
    nj = D_FF // tn
    conv_b2 = conv_b.reshape(1, 2 * D_FF)
    return pl.pallas_call(
        functools.partial(_ffn_kernel, m=m, t=t, nj=nj),
        grid=(rows // m, nj),
        in_specs=[
            pl.BlockSpec((m, d), lambda i, j: (i, 0)),
            pl.BlockSpec((1, N_MOD, d), lambda i, j: (i // blocks_per_mod, 0, 0)),
            pl.BlockSpec(ng.shape, lambda i, j: (0, 0)),
            pl.BlockSpec((d, tn), lambda i, j: (0, j)),
            pl.BlockSpec((d, tn), lambda i, j: (0, nj + j)),
            pl.BlockSpec((FFN_CONV, tn), lambda i, j: (0, j)),
            pl.BlockSpec((FFN_CONV, tn), lambda i, j: (0, nj + j)),
            pl.BlockSpec((1, tn), lambda i, j: (0, j)),
            pl.BlockSpec((1, tn), lambda i, j: (0, nj + j)),
            pl.BlockSpec((tn, d), lambda i, j: (j, 0)),
        ],
        out_specs=pl.BlockSpec((m, d), lambda i, j: (i, 0)),
        out_shape=jax.ShapeDtypeStruct((rows, d), _F32),
        scratch_shapes=[pltpu.VMEM((m, d), _BF16), pltpu.VMEM((m, d), _F32)],
        compiler_params=pltpu.CompilerParams(
            dimension_semantics=("arbitrary", "arbitrary"), vmem_limit_bytes=VMEM_LIMIT),
        name="conv_ffn",
    )(x2, mod, ng, w_up, w_up, conv_w, conv_w, conv_b2, conv_b2, w_down)


def _grid_pos_embedding(t, d):
    rows = t // GRID_W
    row = jnp.repeat(jnp.arange(rows), GRID_W).astype(_F32)
    col = jnp.tile(jnp.arange(GRID_W), rows).astype(_F32)
    quarter = d // 4
    freq = jnp.exp(-math.log(POS_BASE) * jnp.arange(quarter, dtype=_F32) / quarter)

    def enc(p):
        ang = p[:, None] * freq[None, :]
        return jnp.concatenate([jnp.sin(ang), jnp.cos(ang)], axis=-1)

    return jnp.concatenate([enc(row), enc(col)], axis=-1)


def kernel(x_prompt, x_sample, state_lru, c, c_ctx, mod_w, mod_b, norm_g, ab_w_in, lru_conv_w, lru_conv_b, lru_w_gates, lru_b_gates, lru_lambda, conf_conv_w, conf_conv_b, conf_ln_g, conf_ln_b, ab_w_out, c_w_in, c_ln_g, c_ln_b, c_w_s, c_b_s, c_w_out, ffn_w_up, ffn_conv_w, ffn_conv_b, ffn_w_down):
    d = D_MODEL
    batch, seq, _ = x_prompt.shape
    dec_batch, dec_seq, _ = x_sample.shape
    n_ab = ab_w_in.shape[0]

    cond8 = jnp.concatenate(
        [c_ctx[None, :], c, jnp.zeros((MOD_ROWS - 1 - dec_batch, d), _F32)], axis=0)
    mod_tbl = jnp.transpose(_modulation(cond8, mod_w, mod_b), (0, 2, 1, 3))
    mod_p = mod_tbl[:, 0:1]
    mod_s = mod_tbl[:, 1:1 + dec_batch]

    ab_w_in_b = ab_w_in.astype(_BF16)
    ab_w_out_b = ab_w_out.astype(_BF16)
    c_w_in_b = c_w_in.astype(_BF16)
    c_w_out_b = c_w_out.astype(_BF16)
    c_w_s_b = c_w_s.astype(_BF16)
    ffn_w_up_b = ffn_w_up.astype(_BF16)
    ffn_w_down_b = ffn_w_down.astype(_BF16)
    wg_b = jnp.transpose(lru_w_gates, (0, 3, 4, 1, 2, 5)).reshape(
        n_ab, LRU_HEADS, LRU_BLOCK, 4 * LRU_BLOCK).astype(_BF16)
    bg = lru_b_gates.reshape(n_ab, 4, LRU_WIDTH)

    pos = _grid_pos_embedding(dec_seq, d)
    xp = x_prompt.reshape(batch * seq, d)
    xs = x_sample.reshape(dec_batch * dec_seq, d)
    m_p = 2 * seq
    m_s = dec_seq
    m_c = 512

    states = []
    for l in range(DEPTH):
        j = l // 2
        if l % 2 == 0:
            w_ab = (ab_w_in_b[j], lru_conv_w[j], lru_conv_b[j][None], wg_b[j], bg[j],
                    lru_lambda[j], conf_conv_w[j], conf_conv_b[j][None], conf_ln_g[j][None],
                    conf_ln_b[j][None], ab_w_out_b[j])
            xp, st = _mixer_ab(xp, None, mod_p[l], norm_g[l], None, *w_ab,
                               m=m_p, t=seq, mod_per_block=False, emit_state=True)
            states.append(st)
            xs, _ = _mixer_ab(xs, pos if l == 0 else None, mod_s[l], norm_g[l], state_lru[:, j],
                              *w_ab, m=m_s, t=dec_seq, mod_per_block=True, emit_state=False)
        else:
            w_c = (c_w_in_b[j], c_ln_g[j][None], c_ln_b[j][None], c_w_s_b[j], c_b_s[j].T,
                   c_w_out_b[j])
            xp = _mixer_c(xp, mod_p[l], norm_g[l], *w_c, m=m_c,
                          blocks_per_mod=batch * seq // m_c)
            xs = _mixer_c(xs, mod_s[l], norm_g[l], *w_c, m=m_c, blocks_per_mod=dec_seq // m_c)
        w_f = (ffn_w_up_b[l], ffn_conv_w[l], ffn_conv_b[l], ffn_w_down_b[l])
        xp = _conv_ffn(xp, mod_p[l], norm_g[l], *w_f, m=1024, t=seq,
                       blocks_per_mod=batch * seq // 1024)
        xs = _conv_ffn(xs, mod_s[l], norm_g[l], *w_f, m=dec_seq, t=dec_seq, blocks_per_mod=1)

    y_prompt = xp.reshape(batch, seq, d)
    y_sample = xs.reshape(dec_batch, dec_seq, d)
    new_state = jnp.stack(states, axis=1)
    return (y_prompt, y_sample, new_state)
```

```python
import functools
import math

import jax
import jax.numpy as jnp
from jax import lax
from jax.experimental import pallas as pl
from jax.experimental.pallas import tpu as pltpu

D_MODEL = 1024
DEPTH = 4
GRID_W = 64
LRU_WIDTH = 1024
LRU_HEADS = 8
LRU_BLOCK = LRU_WIDTH // LRU_HEADS
LRU_CONV = 4
LRU_C = 8.0
CONF_WIDTH = 1024
CONF_KERNEL = 31
CHUNK = 128
SGU_WIDTH = 2048
SGU_HEADS = 8
SGU_HEAD_DIM = SGU_WIDTH // SGU_HEADS
D_FF = 2816
FFN_CONV = 3
N_MOD = 6
EPS = 1e-6
POS_BASE = 10000.0

SUBLANES = 8
LANES = 128
BF16_ROWS = 16
COL_GROUP = 256
MOD_ROWS = 8
LRU_LEFT = LRU_CONV // 2
CONF_HALF = CONF_KERNEL // 2
CONF_ROWS = 64
FFN_ROWS = 1024
C_ROWS = 512
VMEM_LIMIT = 56 * 1024 * 1024

_BF16 = jnp.bfloat16
_F32 = jnp.float32
_LOG2E = math.log2(math.e)
_GELU_K1 = -2.0 * math.sqrt(2.0 / math.pi) * _LOG2E
_GELU_K2 = _GELU_K1 * 0.044715


def _rms(x, g):
    return x * lax.rsqrt(jnp.mean(x * x, axis=-1, keepdims=True) + EPS) * g


def _dot(a, b):
    return jnp.dot(a, b, preferred_element_type=_F32)


def _sigmoid(x):
    return 1.0 / (1.0 + jnp.exp2(x * (-_LOG2E)))


def _gelu(x):
    return x / (1.0 + jnp.exp2(x * (_GELU_K1 + _GELU_K2 * (x * x))))


def _const_spec(shape):
    zeros = (0,) * len(shape)
    return pl.BlockSpec(shape, lambda *_: zeros, pipeline_mode=pl.Buffered(1))


def _mod_kernel(cond_ref, w_ref, b_ref, o_ref):
    c = cond_ref[...]
    s = (c * _sigmoid(c)).astype(_BF16)
    o_ref[0, 0] = _dot(s, w_ref[0].astype(_BF16)) + b_ref[0, 0]


def _modulation(cond8, mod_w, mod_b):
    d = D_MODEL
    return pl.pallas_call(
        _mod_kernel,
        grid=(DEPTH, N_MOD),
        in_specs=[
            pl.BlockSpec((MOD_ROWS, d), lambda l, n: (0, 0)),
            pl.BlockSpec((1, d, d), lambda l, n: (l, 0, n)),
            pl.BlockSpec((1, 1, 1, d), lambda l, n: (l, n, 0, 0)),
        ],
        out_specs=pl.BlockSpec((1, 1, MOD_ROWS, d), lambda l, n: (l, n, 0, 0)),
        out_shape=jax.ShapeDtypeStruct((DEPTH, N_MOD, MOD_ROWS, d), _F32),
        compiler_params=pltpu.CompilerParams(
            dimension_semantics=("arbitrary", "arbitrary"), vmem_limit_bytes=VMEM_LIMIT),
        name="modulation",
    )(cond8, mod_w, mod_b.reshape(DEPTH, N_MOD, 1, d))


def _ab_kernel(*refs, m, t, has_pos, has_h0, emit_state):
    nseq = m // t
    lc = t // SUBLANES
    nplane = D_MODEL // LANES
    refs = list(refs)
    x_ref = refs.pop(0)
    pos_ref = refs.pop(0) if has_pos else None
    mod_ref = refs.pop(0)
    ng_ref = refs.pop(0)
    h0_ref = refs.pop(0) if has_h0 else None
    (w_in_ref, cw_ref, cb_ref, wg_ref, bg_ref, lam_ref,
     ccw_ref, ccb_ref, lng_ref, lnb_ref, w_out_ref) = refs[:11]
    refs = refs[11:]
    xo_ref = refs.pop(0)
    st_ref = refs.pop(0) if emit_state else None
    (h_scr, ycat_scr, plane_scr, af_scr, bf_scr, ab_scr, bb_scr, xext_scr, gext_scr) = refs

    if has_pos:
        xo_ref[...] = x_ref[...] + pos_ref[...]
    else:
        xo_ref[...] = x_ref[...]
    mod = mod_ref[0]
    sh1, sc1, g1 = mod[0:1], mod[1:2], mod[2:3]

    hn = _rms(xo_ref[...], ng_ref[0:1]) * (1.0 + sc1) + sh1
    for p in range(nplane):
        plane_scr[p] = hn[:, p * LANES:(p + 1) * LANES]

    def permute_body(i, _):
        k = i * 2
        for q in range(nseq):
            dst = pl.multiple_of(q * t + k * SUBLANES, BF16_ROWS)
            for p in range(nplane):
                v0 = plane_scr[p, pl.ds(q * t + k, SUBLANES, stride=lc), :]
                v1 = plane_scr[p, pl.ds(q * t + k + 1, SUBLANES, stride=lc), :]
                h_scr[pl.ds(dst, BF16_ROWS), p * LANES:(p + 1) * LANES] = (
                    jnp.concatenate([v0, v1], axis=0).astype(_BF16))
        return 0

    lax.fori_loop(0, lc // 2, permute_body, 0)

    neg_lam = -lam_ref[...]
    softplus = jnp.maximum(neg_lam, 0.0) + jnp.log1p(jnp.exp(-jnp.abs(neg_lam)))
    decay = softplus * (-LRU_C * _LOG2E)

    sub = lax.broadcasted_iota(jnp.int32, (SUBLANES, COL_GROUP), 0)

    def front_halo(v, ngroups):
        s_idx = lax.broadcasted_iota(jnp.int32, v.shape, 0) & (SUBLANES - 1)
        return jnp.where(s_idx == 0, 0.0, pltpu.roll(v, 1, 0))

    def back_halo(v, ngroups):
        s_idx = lax.broadcasted_iota(jnp.int32, v.shape, 0) & (SUBLANES - 1)
        return jnp.where(s_idx == SUBLANES - 1, 0.0, pltpu.roll(v, v.shape[0] - 1, 0))

    def fill_extended(ext_scr, val, halo):
        hr = halo * SUBLANES
        for q in range(nseq):
            seq = val[q * t:(q + 1) * t]
            ext_scr[q, hr:hr + t, :] = seq
            ext_scr[q, 0:hr, :] = front_halo(seq[t - hr:], halo)
            ext_scr[q, hr + t:, :] = back_halo(seq[:hr], halo)

    for grp in range(LRU_WIDTH // COL_GROUP):
        c0 = grp * COL_GROUP
        cols = slice(c0, c0 + COL_GROUP)
        h = h_scr[...]

        zx = _dot(h, w_in_ref[:, LRU_WIDTH + c0:LRU_WIDTH + c0 + COL_GROUP])
        fill_extended(xext_scr, zx, LRU_LEFT)
        xc_parts = []
        for q in range(nseq):
            acc = jnp.broadcast_to(cb_ref[:, cols], (t, COL_GROUP))
            for k in range(LRU_CONV):
                off = k * SUBLANES
                acc = acc + cw_ref[k:k + 1, cols] * xext_scr[q, off:off + t, :]
            xc_parts.append(acc)
        xc = xc_parts[0] if nseq == 1 else jnp.concatenate(xc_parts, axis=0)

        xcb = xc.astype(_BF16)
        heads_per_group = COL_GROUP // LRU_BLOCK
        gl_parts = []
        for hh in range(heads_per_group):
            head = grp * heads_per_group + hh
            gl_parts.append(_dot(xcb[:, hh * LRU_BLOCK:(hh + 1) * LRU_BLOCK], wg_ref[head]))

        def gate(dg):
            parts = [gp[:, dg * LRU_BLOCK:(dg + 1) * LRU_BLOCK] for gp in gl_parts]
            return _sigmoid(jnp.concatenate(parts, axis=1) + bg_ref[dg:dg + 1, cols])

        for d, (a_scr, b_scr) in enumerate(((af_scr, bf_scr), (ab_scr, bb_scr))):
            r = gate(2 * d)
            i_gate = gate(2 * d + 1)
            a = jnp.exp2(r * decay[d:d + 1, cols])
            y = 1.0 - a * a
            a_scr[...] = a
            b_scr[...] = jnp.where(y > 0.0, y * lax.rsqrt(y), 0.0) * i_gate * xc

        def scan_body(i, carry):
            out = []
            for q in range(nseq):
                hf, df, hb, db = carry[q]
                rows_f = pl.ds(pl.multiple_of(q * t + i * SUBLANES, SUBLANES), SUBLANES)
                rows_b = pl.ds(pl.multiple_of(q * t + (lc - 1 - i) * SUBLANES, SUBLANES), SUBLANES)
                a = af_scr[rows_f, :]
                hf = a * hf + bf_scr[rows_f, :]
                df = df * a
                bf_scr[rows_f, :] = hf
                af_scr[rows_f, :] = df
                a = ab_scr[rows_b, :]
                hb = a * hb + bb_scr[rows_b, :]
                db = db * a
                bb_scr[rows_b, :] = hb
                ab_scr[rows_b, :] = db
                out.append((hf, df, hb, db))
            return tuple(out)

        zeros8 = jnp.zeros((SUBLANES, COL_GROUP), _F32)
        ones8 = jnp.ones((SUBLANES, COL_GROUP), _F32)
        ends = lax.fori_loop(0, lc, scan_body,
                             tuple((zeros8, ones8, zeros8, ones8) for _ in range(nseq)))

        zg = _dot(h, w_in_ref[:, cols])
        gz = _gelu(zg)
        for q in range(nseq):
            hf_end, df_end, hb_end, db_end = ends[q]
            if has_h0:
                cf = h0_ref[q, 0:1, cols]
                cb = h0_ref[q, 1:2, cols]
            else:
                cf = jnp.zeros((1, COL_GROUP), _F32)
                cb = cf
            cf_all = jnp.zeros((SUBLANES, COL_GROUP), _F32)
            for s in range(SUBLANES):
                cf_all = jnp.where(sub == s, cf, cf_all)
                cf = hf_end[s:s + 1] + df_end[s:s + 1] * cf
            cb_all = jnp.zeros((SUBLANES, COL_GROUP), _F32)
            for s in reversed(range(SUBLANES)):
                cb_all = jnp.where(sub == s, cb, cb_all)
                cb = hb_end[s:s + 1] + db_end[s:s + 1] * cb
            if emit_state:
                st_ref[q, 0:1, cols] = cf
                st_ref[q, 1:2, cols] = cb
            rows = slice(q * t, (q + 1) * t)
            shape3 = (lc, SUBLANES, COL_GROUP)
            hsum = (bf_scr[rows, :].reshape(shape3) + af_scr[rows, :].reshape(shape3) * cf_all[None]
                    + bb_scr[rows, :].reshape(shape3) + ab_scr[rows, :].reshape(shape3) * cb_all[None])
            ycat_scr[rows, cols] = (gz[rows] * hsum.reshape(t, COL_GROUP)).astype(_BF16)

        za = _dot(h, w_in_ref[:, 2 * LRU_WIDTH + c0:2 * LRU_WIDTH + c0 + COL_GROUP])
        zb = _dot(h, w_in_ref[:, 2 * LRU_WIDTH + CONF_WIDTH + c0:
                              2 * LRU_WIDTH + CONF_WIDTH + c0 + COL_GROUP])
        fill_extended(gext_scr, za * _sigmoid(zb), CONF_HALF)

        def conv_body(i, _):
            r0 = pl.multiple_of(i * CONF_ROWS, CONF_ROWS)
            for q in range(nseq):
                acc = jnp.broadcast_to(ccb_ref[:, cols], (CONF_ROWS, COL_GROUP))
                for k in range(CONF_KERNEL):
                    acc = acc + ccw_ref[k:k + 1, cols] * gext_scr[
                        q, pl.ds(r0 + k * SUBLANES, CONF_ROWS), :]
                for v in range(COL_GROUP // LANES):
                    plane_scr[grp * (COL_GROUP // LANES) + v, pl.ds(q * t + r0, CONF_ROWS), :] = (
                        acc[:, v * LANES:(v + 1) * LANES])
            return 0

        lax.fori_loop(0, t // CONF_ROWS, conv_body, 0)

    gc = jnp.concatenate([plane_scr[p] for p in range(nplane)], axis=1)
    mu = jnp.mean(gc, axis=-1, keepdims=True)
    gcc = gc - mu
    var = jnp.mean(gcc * gcc, axis=-1, keepdims=True)
    yb = gcc * lax.rsqrt(var + EPS) * lng_ref[...] + lnb_ref[...]
    ycat_scr[:, LRU_WIDTH:] = (yb * _sigmoid(yb)).astype(_BF16)

    o = _dot(ycat_scr[...], w_out_ref[...])
    upd = g1 * _rms(o, ng_ref[1:2])
    for p in range(nplane):
        plane_scr[p] = upd[:, p * LANES:(p + 1) * LANES]

    def unpermute_body(i, _):
        for q in range(nseq):
            for s in range(SUBLANES):
                dst = pl.ds(pl.multiple_of(q * t + s * lc + i * SUBLANES, SUBLANES), SUBLANES)
                for p in range(nplane):
                    u = plane_scr[p, pl.ds(q * t + i * SUBLANES * SUBLANES + s, SUBLANES,
                                          stride=SUBLANES), :]
                    pc = slice(p * LANES, (p + 1) * LANES)
                    xo_ref[dst, pc] = xo_ref[dst, pc] + u
        return 0

    lax.fori_loop(0, lc // SUBLANES, unpermute_body, 0)


def _mixer_ab(x_all, pos, mod, ng, h0, w_in, cw, cb, wg, bg, lam, ccw, ccb, lng, lnb, w_out,
              *, m, t, row0, nblk, mod_row0, emit_state):
    rows, d = x_all.shape
    nseq = m // t
    blk0 = row0 // m
    has_pos = pos is not None
    has_h0 = h0 is not None
    args = [x_all]
    in_specs = [pl.BlockSpec((m, d), lambda i: (i + blk0, 0), pipeline_mode=pl.Buffered(1))]
    if has_pos:
        args.append(pos)
        in_specs.append(_const_spec(pos.shape))
    args.append(mod)
    in_specs.append(pl.BlockSpec((1, N_MOD, d), (lambda i: (i + mod_row0, 0, 0)) if has_h0
                                 else (lambda i: (mod_row0, 0, 0))))
    args.append(ng)
    in_specs.append(_const_spec(ng.shape))
    if has_h0:
        args.append(h0)
        in_specs.append(pl.BlockSpec((nseq, 2, LRU_WIDTH), lambda i: (i, 0, 0)))
    for w in (w_in, cw, cb, wg, bg, lam, ccw, ccb, lng, lnb, w_out):
        args.append(w)
        in_specs.append(_const_spec(w.shape))
    out_shape = [jax.ShapeDtypeStruct((rows, d), _F32)]
    out_specs = [pl.BlockSpec((m, d), lambda i: (i + blk0, 0))]
    if emit_state:
        out_shape.append(jax.ShapeDtypeStruct((nblk * nseq, 2, LRU_WIDTH), _F32))
        out_specs.append(pl.BlockSpec((nseq, 2, LRU_WIDTH), lambda i: (i, 0, 0)))
    scratch = [
        pltpu.VMEM((m, d), _BF16),
        pltpu.VMEM((m, LRU_WIDTH + CONF_WIDTH), _BF16),
        pltpu.VMEM((d // LANES, m, LANES), _F32),
        pltpu.VMEM((m, COL_GROUP), _F32),
        pltpu.VMEM((m, COL_GROUP), _F32),
        pltpu.VMEM((m, COL_GROUP), _F32),
        pltpu.VMEM((m, COL_GROUP), _F32),
        pltpu.VMEM((nseq, t + 2 * LRU_LEFT * SUBLANES, COL_GROUP), _F32),
        pltpu.VMEM((nseq, t + 2 * CONF_HALF * SUBLANES, COL_GROUP), _F32),
    ]
    res = pl.pallas_call(
        functools.partial(_ab_kernel, m=m, t=t, has_pos=has_pos, has_h0=has_h0,
                          emit_state=emit_state),
        grid=(nblk,),
        in_specs=in_specs,
        out_specs=out_specs,
        out_shape=out_shape,
        scratch_shapes=scratch,
        input_output_aliases={0: 0},
        compiler_params=pltpu.CompilerParams(
            dimension_semantics=("arbitrary",), vmem_limit_bytes=VMEM_LIMIT),
        name="mixer_ab",
    )(*args)
    return res if emit_state else (res[0], None)


def _c_kernel(x_ref, mod_ref, ng_ref, w_in_ref, lng_ref, lnb_ref, ws_ref, bs_ref, w_out_ref,
              xo_ref, h_scr, v_scr, y_scr, *, m):
    mod = mod_ref[0]
    sh1, sc1, g1 = mod[0:1], mod[1:2], mod[2:3]
    h_scr[...] = (_rms(x_ref[...], ng_ref[0:1]) * (1.0 + sc1) + sh1).astype(_BF16)
    h = h_scr[...]

    for hd in range(SGU_HEADS):
        c0 = SGU_WIDTH + hd * SGU_HEAD_DIM
        v_scr[:, hd * SGU_HEAD_DIM:(hd + 1) * SGU_HEAD_DIM] = _gelu(
            _dot(h, w_in_ref[:, c0:c0 + SGU_HEAD_DIM]))
    v = v_scr[...]
    mu = jnp.mean(v, axis=-1, keepdims=True)
    vc = v - mu
    var = jnp.mean(vc * vc, axis=-1, keepdims=True)
    rstd = lax.rsqrt(var + EPS)

    for hd in range(SGU_HEADS):
        cols = slice(hd * SGU_HEAD_DIM, (hd + 1) * SGU_HEAD_DIM)
        vn = ((v_scr[:, cols] - mu) * rstd * lng_ref[:, cols] + lnb_ref[:, cols]).astype(_BF16)
        u = _gelu(_dot(h, w_in_ref[:, cols]))
        w_s = ws_ref[hd]
        bias = bs_ref[:, hd:hd + 1]
        for n in range(m // CHUNK):
            rows = slice(n * CHUNK, (n + 1) * CHUNK)
            s = _dot(w_s, vn[rows]) + bias
            y_scr[rows, cols] = (u[rows] * s).astype(_BF16)

    o = _dot(y_scr[...], w_out_ref[...])
    xo_ref[...] = x_ref[...] + g1 * _rms(o, ng_ref[1:2])


def _mixer_c(x_all, mod, ng, w_in, lng, lnb, ws, bs_t, w_out, *, m, mod_index):
    rows, d = x_all.shape
    in_specs = [
        pl.BlockSpec((m, d), lambda i: (i, 0)),
        pl.BlockSpec((1, N_MOD, d), lambda i: (mod_index(i), 0, 0)),
    ] + [_const_spec(w.shape) for w in (ng, w_in, lng, lnb, ws, bs_t, w_out)]
    return pl.pallas_call(
        functools.partial(_c_kernel, m=m),
        grid=(rows // m,),
        in_specs=in_specs,
        out_specs=pl.BlockSpec((m, d), lambda i: (i, 0)),
        out_shape=jax.ShapeDtypeStruct((rows, d), _F32),
        scratch_shapes=[
            pltpu.VMEM((m, d), _BF16),
            pltpu.VMEM((m, SGU_WIDTH), _F32),
            pltpu.VMEM((m, SGU_WIDTH), _BF16),
        ],
        compiler_params=pltpu.CompilerParams(
            dimension_semantics=("arbitrary",), vmem_limit_bytes=VMEM_LIMIT),
        name="mixer_c",
    )(x_all, mod, ng, w_in, lng, lnb, ws, bs_t, w_out)


def _ffn_kernel(x_ref, mod_ref, ng_ref, wup_ref, cw_ref, cb_ref, wd_ref, xo_ref, h_scr, act_scr,
                *, m, seg, n_short_blocks, t_long):
    long_seq = pl.program_id(0) >= n_short_blocks
    mod = mod_ref[0]
    sh2, sc2, g2 = mod[3:4], mod[4:5], mod[5:6]
    h_scr[...] = (_rms(x_ref[...], ng_ref[2:3]) * (1.0 + sc2) + sh2).astype(_BF16)
    h = h_scr[...]
    tn = COL_GROUP
    sub = lax.broadcasted_iota(jnp.int32, (SUBLANES, tn), 0)

    def conv(z, c0):
        outs = []
        for g in range(m // seg):
            zs = z[g * seg:(g + 1) * seg]
            prev = pltpu.roll(zs, 1, 0)
            nxt = pltpu.roll(zs, seg - 1, 0)
            if g > 0 and t_long > seg:
                before = jnp.where(long_seq, z[g * seg - 1:g * seg], 0.0)
            else:
                before = jnp.zeros((1, tn), _F32)
            if g < m // seg - 1 and t_long > seg:
                after = jnp.where(long_seq, z[(g + 1) * seg:(g + 1) * seg + 1], 0.0)
            else:
                after = jnp.zeros((1, tn), _F32)
            prev = jnp.concatenate(
                [jnp.where(sub == 0, before, prev[:SUBLANES]), prev[SUBLANES:]], axis=0)
            nxt = jnp.concatenate(
                [nxt[:seg - SUBLANES],
                 jnp.where(sub == SUBLANES - 1, after, nxt[seg - SUBLANES:])], axis=0)
            cols = slice(c0, c0 + tn)
            outs.append(cw_ref[0:1, cols] * prev + cw_ref[1:2, cols] * zs
                        + cw_ref[2:3, cols] * nxt + cb_ref[:, cols])
        return jnp.concatenate(outs, axis=0)

    for j in range(D_FF // tn):
        gate = conv(_dot(h, wup_ref[:, j * tn:(j + 1) * tn]), j * tn)
        val = conv(_dot(h, wup_ref[:, D_FF + j * tn:D_FF + (j + 1) * tn]), D_FF + j * tn)
        act_scr[:, j * tn:(j + 1) * tn] = (_gelu(gate) * val).astype(_BF16)

    o = _dot(act_scr[...], wd_ref[...])
    xo_ref[...] = x_ref[...] + g2 * _rms(o, ng_ref[3:4])


def _conv_ffn(x_all, mod, ng, w_up, conv_w, conv_b, w_down, *, m, seg, n_short_blocks, t_long,
              mod_index):
    rows, d = x_all.shape
    conv_b2 = conv_b.reshape(1, 2 * D_FF)
    return pl.pallas_call(
        functools.partial(_ffn_kernel, m=m, seg=seg, n_short_blocks=n_short_blocks,
                          t_long=t_long),
        grid=(rows // m,),
        in_specs=[
            pl.BlockSpec((m, d), lambda i: (i, 0), pipeline_mode=pl.Buffered(1)),
            pl.BlockSpec((1, N_MOD, d), lambda i: (mod_index(i), 0, 0)),
        ] + [_const_spec(w.shape) for w in (ng, w_up, conv_w, conv_b2, w_down)],
        out_specs=pl.BlockSpec((m, d), lambda i: (i, 0)),
        out_shape=jax.ShapeDtypeStruct((rows, d), _F32),
        scratch_shapes=[pltpu.VMEM((m, d), _BF16), pltpu.VMEM((m, D_FF), _BF16)],
        compiler_params=pltpu.CompilerParams(
            dimension_semantics=("arbitrary",), vmem_limit_bytes=VMEM_LIMIT),
        name="conv_ffn",
    )(x_all, mod, ng, w_up, conv_w, conv_b2, w_down)


def _grid_pos_embedding(t, d):
    rows = t // GRID_W
    row = jnp.repeat(jnp.arange(rows), GRID_W).astype(_F32)
    col = jnp.tile(jnp.arange(GRID_W), rows).astype(_F32)
    quarter = d // 4
    freq = jnp.exp(-math.log(POS_BASE) * jnp.arange(quarter, dtype=_F32) / quarter)

    def enc(p):
        ang = p[:, None] * freq[None, :]
        return jnp.concatenate([jnp.sin(ang), jnp.cos(ang)], axis=-1)

    return jnp.concatenate([enc(row), enc(col)], axis=-1)


def kernel(x_prompt, x_sample, state_lru, c, c_ctx, mod_w, mod_b, norm_g, ab_w_in, lru_conv_w, lru_conv_b, lru_w_gates, lru_b_gates, lru_lambda, conf_conv_w, conf_conv_b, conf_ln_g, conf_ln_b, ab_w_out, c_w_in, c_ln_g, c_ln_b, c_w_s, c_b_s, c_w_out, ffn_w_up, ffn_conv_w, ffn_conv_b, ffn_w_down):
    d = D_MODEL
    batch, seq, _ = x_prompt.shape
    dec_batch, dec_seq, _ = x_sample.shape
    n_ab = ab_w_in.shape[0]
    rows_p = batch * seq
    rows_s = dec_batch * dec_seq

    cond8 = jnp.concatenate(
        [c_ctx[None, :], c, jnp.zeros((MOD_ROWS - 1 - dec_batch, d), _F32)], axis=0)
    mod_tbl = jnp.transpose(_modulation(cond8, mod_w, mod_b), (0, 2, 1, 3))

    def mod_index(block_rows):
        first_latent = rows_p // block_rows
        per_seq = dec_seq // block_rows
        return lambda i: jnp.maximum((i - first_latent) // per_seq + 1, 0)

    ab_w_in_b = ab_w_in.astype(_BF16)
    ab_w_out_b = ab_w_out.astype(_BF16)
    c_w_in_b = c_w_in.astype(_BF16)
    c_w_out_b = c_w_out.astype(_BF16)
    c_w_s_b = c_w_s.astype(_BF16)
    ffn_w_up_b = ffn_w_up.astype(_BF16)
    ffn_w_down_b = ffn_w_down.astype(_BF16)
    wg_b = jnp.transpose(lru_w_gates, (0, 3, 4, 1, 2, 5)).reshape(
        n_ab, LRU_HEADS, LRU_BLOCK, 4 * LRU_BLOCK).astype(_BF16)
    bg = lru_b_gates.reshape(n_ab, 4, LRU_WIDTH)

    pos = _grid_pos_embedding(dec_seq, d)
    x_all = jnp.concatenate([x_prompt.reshape(rows_p, d), x_sample.reshape(rows_s, d)], axis=0)
    m_p = 2 * seq
    m_s = dec_seq

    states = []
    for l in range(DEPTH):
        j = l // 2
        if l % 2 == 0:
            w_ab = (ab_w_in_b[j], lru_conv_w[j], lru_conv_b[j][None], wg_b[j], bg[j],
                    lru_lambda[j], conf_conv_w[j], conf_conv_b[j][None], conf_ln_g[j][None],
                    conf_ln_b[j][None], ab_w_out_b[j])
            x_all, st = _mixer_ab(x_all, None, mod_tbl[l], norm_g[l], None, *w_ab,
                                  m=m_p, t=seq, row0=0, nblk=rows_p // m_p, mod_row0=0,
                                  emit_state=True)
            states.append(st)
            x_all, _ = _mixer_ab(x_all, pos if l == 0 else None, mod_tbl[l], norm_g[l],
                                 state_lru[:, j], *w_ab, m=m_s, t=dec_seq, row0=rows_p,
                                 nblk=dec_batch, mod_row0=1, emit_state=False)
        else:
            x_all = _mixer_c(x_all, mod_tbl[l], norm_g[l], c_w_in_b[j], c_ln_g[j][None],
                             c_ln_b[j][None], c_w_s_b[j], c_b_s[j].T, c_w_out_b[j],
                             m=C_ROWS, mod_index=mod_index(C_ROWS))
        x_all = _conv_ffn(x_all, mod_tbl[l], norm_g[l], ffn_w_up_b[l], ffn_conv_w[l],
                          ffn_conv_b[l], ffn_w_down_b[l], m=FFN_ROWS, seg=seq,
                          n_short_blocks=rows_p // FFN_ROWS, t_long=dec_seq,
                          mod_index=mod_index(FFN_ROWS))

    y_prompt = x_all[:rows_p].reshape(batch, seq, d)
    y_sample = x_all[rows_p:].reshape(dec_batch, dec_seq, d)
    new_state = jnp.stack(states, axis=1)
    return (y_prompt, y_sample, new_state)
```

```python
import functools
import math

import jax
import jax.numpy as jnp
from jax import lax
from jax.experimental import pallas as pl
from jax.experimental.pallas import tpu as pltpu

D_MODEL = 1024
DEPTH = 4
GRID_W = 64
LRU_WIDTH = 1024
LRU_HEADS = 8
LRU_BLOCK = LRU_WIDTH // LRU_HEADS
LRU_CONV = 4
LRU_C = 8.0
CONF_WIDTH = 1024
CONF_KERNEL = 31
CHUNK = 128
SGU_WIDTH = 2048
SGU_HEADS = 8
SGU_HEAD_DIM = SGU_WIDTH // SGU_HEADS
D_FF = 2816
FFN_CONV = 3
N_MOD = 6
EPS = 1e-6
POS_BASE = 10000.0

SUBLANES = 8
LANES = 128
BF16_ROWS = 16
COL_GROUP = 256
N_GROUPS = LRU_WIDTH // COL_GROUP
MOD_ROWS = 8
LRU_LEFT = LRU_CONV // 2
CONF_HALF = CONF_KERNEL // 2
CONF_ROWS = 32
ROW_CHUNK = 256
FFN_ROWS = 1024
C_ROWS = 512
VMEM_LIMIT = 56 * 1024 * 1024

_BF16 = jnp.bfloat16
_F32 = jnp.float32
_LOG2E = math.log2(math.e)
_GELU_K1 = -2.0 * math.sqrt(2.0 / math.pi) * _LOG2E
_GELU_K2 = _GELU_K1 * 0.044715


def _rms(x, g):
    return x * lax.rsqrt(jnp.mean(x * x, axis=-1, keepdims=True) + EPS) * g


def _dot(a, b):
    return jnp.dot(a, b, preferred_element_type=_F32)


def _sigmoid(x):
    return 1.0 / (1.0 + jnp.exp2(x * (-_LOG2E)))


def _gelu(x):
    return x / (1.0 + jnp.exp2(x * (_GELU_K1 + _GELU_K2 * (x * x))))


def _layer_spec(arr, layer):
    tail = (0,) * (arr.ndim - 1)
    return pl.BlockSpec((None,) + arr.shape[1:], lambda *_: (layer,) + tail,
                        pipeline_mode=pl.Buffered(1))


def _mod_kernel(cond_ref, w_ref, b_ref, o_ref):
    c = cond_ref[...]
    s = (c * _sigmoid(c)).astype(_BF16)
    o_ref[0, 0] = _dot(s, w_ref[0].astype(_BF16)) + b_ref[0, 0]


def _modulation(cond8, mod_w, mod_b):
    d = D_MODEL
    return pl.pallas_call(
        _mod_kernel,
        grid=(DEPTH, N_MOD),
        in_specs=[
            pl.BlockSpec((MOD_ROWS, d), lambda l, n: (0, 0)),
            pl.BlockSpec((1, d, d), lambda l, n: (l, 0, n)),
            pl.BlockSpec((1, 1, 1, d), lambda l, n: (l, n, 0, 0)),
        ],
        out_specs=pl.BlockSpec((1, 1, MOD_ROWS, d), lambda l, n: (l, n, 0, 0)),
        out_shape=jax.ShapeDtypeStruct((DEPTH, N_MOD, MOD_ROWS, d), _F32),
        compiler_params=pltpu.CompilerParams(
            dimension_semantics=("arbitrary", "arbitrary"), vmem_limit_bytes=VMEM_LIMIT),
        name="modulation",
    )(cond8, mod_w, mod_b.reshape(DEPTH, N_MOD, 1, d))


def _ab_kernel(*refs, m, t, has_pos, has_h0, emit_state):
    nseq = m // t
    lc = t // SUBLANES
    nplane = D_MODEL // LANES
    rc = min(ROW_CHUNK, t)
    refs = list(refs)
    x_ref = refs.pop(0)
    pos_ref = refs.pop(0) if has_pos else None
    mod_ref = refs.pop(0)
    ng_ref = refs.pop(0)
    h0_ref = refs.pop(0) if has_h0 else None
    (w_in_ref, cw_ref, cb_ref, wg_ref, bg_ref, lam_ref,
     ccw_ref, ccb_ref, lng_ref, lnb_ref, w_out_ref) = refs[:11]
    refs = refs[11:]
    xo_ref = refs.pop(0)
    st_ref = refs.pop(0) if emit_state else None
    (h_scr, ycat_scr, plane_scr, af_scr, bf_scr, ab_scr, bb_scr, zg_scr, za_scr, zb_scr,
     xext_scr, gext_scr, wtap_scr) = refs

    if has_pos:
        xo_ref[...] = x_ref[...] + pos_ref[...]
    else:
        xo_ref[...] = x_ref[...]
    mod = mod_ref[...]
    sh1, sc1, g1 = mod[0:1], mod[1:2], mod[2:3]

    for r0 in range(0, m, rc):
        hn = _rms(xo_ref[r0:r0 + rc, :], ng_ref[0:1]) * (1.0 + sc1) + sh1
        for p in range(nplane):
            plane_scr[p, r0:r0 + rc, :] = hn[:, p * LANES:(p + 1) * LANES]

    def permute_body(i, _):
        k = i * 2
        for q in range(nseq):
            dst = pl.multiple_of(q * t + k * SUBLANES, BF16_ROWS)
            for p in range(nplane):
                v0 = plane_scr[p, pl.ds(q * t + k, SUBLANES, stride=lc), :]
                v1 = plane_scr[p, pl.ds(q * t + k + 1, SUBLANES, stride=lc), :]
                h_scr[pl.ds(dst, BF16_ROWS), p * LANES:(p + 1) * LANES] = (
                    jnp.concatenate([v0, v1], axis=0).astype(_BF16))
        return 0

    lax.fori_loop(0, lc // 2, permute_body, 0)

    neg_lam = -lam_ref[...]
    softplus = jnp.maximum(neg_lam, 0.0) + jnp.log1p(jnp.exp(-jnp.abs(neg_lam)))
    decay = softplus * (-LRU_C * _LOG2E)

    sub = lax.broadcasted_iota(jnp.int32, (SUBLANES, COL_GROUP), 0)

    def front_halo(v):
        s_idx = lax.broadcasted_iota(jnp.int32, v.shape, 0) & (SUBLANES - 1)
        return jnp.where(s_idx == 0, 0.0, pltpu.roll(v, 1, 0))

    def back_halo(v):
        s_idx = lax.broadcasted_iota(jnp.int32, v.shape, 0) & (SUBLANES - 1)
        return jnp.where(s_idx == SUBLANES - 1, 0.0, pltpu.roll(v, v.shape[0] - 1, 0))

    def fill_halos(ext_scr, halo):
        hr = halo * SUBLANES
        for q in range(nseq):
            ext_scr[q, 0:hr, :] = front_halo(ext_scr[q, t:t + hr, :])
            ext_scr[q, hr + t:, :] = back_halo(ext_scr[q, hr:2 * hr, :])

    def project(dst_ref, row_slice, r0, col0):
        dst_ref[row_slice] = _dot(h_scr[r0:r0 + rc, :], w_in_ref[:, col0:col0 + COL_GROUP])

    def project_x(grp, r0):
        q, rq = divmod(r0, t)
        hr = LRU_LEFT * SUBLANES
        project(xext_scr, (q, slice(hr + rq, hr + rq + rc), slice(None)), r0,
                LRU_WIDTH + grp * COL_GROUP)

    for r0 in range(0, m, rc):
        project_x(0, r0)
    fill_halos(xext_scr, LRU_LEFT)

    for grp in range(N_GROUPS):
        c0 = grp * COL_GROUP
        cols = slice(c0, c0 + COL_GROUP)

        heads_per_group = COL_GROUP // LRU_BLOCK
        for r0 in range(0, m, rc):
            q, rq = divmod(r0, t)
            project(zg_scr, slice(r0, r0 + rc), r0, c0)
            project(za_scr, slice(r0, r0 + rc), r0, 2 * LRU_WIDTH + c0)
            project(zb_scr, slice(r0, r0 + rc), r0, 2 * LRU_WIDTH + CONF_WIDTH + c0)
            xc = jnp.broadcast_to(cb_ref[:, cols], (rc, COL_GROUP))
            for k in range(LRU_CONV):
                off = rq + k * SUBLANES
                xc = xc + cw_ref[k:k + 1, cols] * xext_scr[q, off:off + rc, :]
            xcb = xc.astype(_BF16)
            gl_parts = [
                _dot(xcb[:, hh * LRU_BLOCK:(hh + 1) * LRU_BLOCK],
                     wg_ref[grp * heads_per_group + hh])
                for hh in range(heads_per_group)]

            def gate(dg):
                parts = [gp[:, dg * LRU_BLOCK:(dg + 1) * LRU_BLOCK] for gp in gl_parts]
                return _sigmoid(jnp.concatenate(parts, axis=1) + bg_ref[dg:dg + 1, cols])

            for d, (a_scr, b_scr) in enumerate(((af_scr, bf_scr), (ab_scr, bb_scr))):
                r = gate(2 * d)
                i_gate = gate(2 * d + 1)
                a = jnp.exp2(r * decay[d:d + 1, cols])
                y = 1.0 - a * a
                a_scr[r0:r0 + rc, :] = a
                b_scr[r0:r0 + rc, :] = jnp.where(y > 0.0, y * lax.rsqrt(y), 0.0) * i_gate * xc

        def scan_body(i, carry):
            out = []
            for q in range(nseq):
                hf, df, hb, db = carry[q]
                rows_f = pl.ds(pl.multiple_of(q * t + i * SUBLANES, SUBLANES), SUBLANES)
                rows_b = pl.ds(pl.multiple_of(q * t + (lc - 1 - i) * SUBLANES, SUBLANES), SUBLANES)
                a = af_scr[rows_f, :]
                hf = a * hf + bf_scr[rows_f, :]
                df = df * a
                bf_scr[rows_f, :] = hf
                af_scr[rows_f, :] = df
                a = ab_scr[rows_b, :]
                hb = a * hb + bb_scr[rows_b, :]
                db = db * a
                bb_scr[rows_b, :] = hb
                ab_scr[rows_b, :] = db
                out.append((hf, df, hb, db))
            return tuple(out)

        zeros8 = jnp.zeros((SUBLANES, COL_GROUP), _F32)
        ones8 = jnp.ones((SUBLANES, COL_GROUP), _F32)
        ends = lax.fori_loop(0, lc, scan_body,
                             tuple((zeros8, ones8, zeros8, ones8) for _ in range(nseq)))

        entry = []
        for q in range(nseq):
            hf_end, df_end, hb_end, db_end = ends[q]
            if has_h0:
                cf = h0_ref[q, 0:1, cols]
                cb = h0_ref[q, 1:2, cols]
            else:
                cf = jnp.zeros((1, COL_GROUP), _F32)
                cb = cf
            cf_all = jnp.zeros((SUBLANES, COL_GROUP), _F32)
            for s in range(SUBLANES):
                cf_all = jnp.where(sub == s, cf, cf_all)
                cf = hf_end[s:s + 1] + df_end[s:s + 1] * cf
            cb_all = jnp.zeros((SUBLANES, COL_GROUP), _F32)
            for s in reversed(range(SUBLANES)):
                cb_all = jnp.where(sub == s, cb, cb_all)
                cb = hb_end[s:s + 1] + db_end[s:s + 1] * cb
            if emit_state:
                st_ref[q, 0:1, cols] = cf
                st_ref[q, 1:2, cols] = cb
            entry.append((cf_all, cb_all))
        gr = CONF_HALF * SUBLANES
        for r0 in range(0, m, rc):
            q, rq = divmod(r0, t)
            rows = slice(r0, r0 + rc)
            if grp + 1 < N_GROUPS:
                project_x(grp + 1, r0)
            cf_all, cb_all = entry[q]
            shape3 = (rc // SUBLANES, SUBLANES, COL_GROUP)
            hsum = (bf_scr[rows, :].reshape(shape3) + af_scr[rows, :].reshape(shape3) * cf_all[None]
                    + bb_scr[rows, :].reshape(shape3) + ab_scr[rows, :].reshape(shape3) * cb_all[None])
            ycat_scr[rows, cols] = (
                _gelu(zg_scr[rows, :]) * hsum.reshape(rc, COL_GROUP)).astype(_BF16)
            gext_scr[q, gr + rq:gr + rq + rc, :] = za_scr[rows, :] * _sigmoid(zb_scr[rows, :])
        fill_halos(gext_scr, CONF_HALF)
        if grp + 1 < N_GROUPS:
            fill_halos(xext_scr, LRU_LEFT)

        for k in range(CONF_KERNEL):
            wtap_scr[k] = jnp.broadcast_to(ccw_ref[k:k + 1, cols], (SUBLANES, COL_GROUP))
        shape3 = (CONF_ROWS // SUBLANES, SUBLANES, COL_GROUP)

        def conv_body(i, _):
            r0 = pl.multiple_of(i * CONF_ROWS, CONF_ROWS)
            for q in range(nseq):
                acc = jnp.broadcast_to(ccb_ref[:, cols], shape3)
                for k in range(CONF_KERNEL):
                    tile = gext_scr[q, pl.ds(r0 + k * SUBLANES, CONF_ROWS), :]
                    acc = acc + wtap_scr[k][None] * tile.reshape(shape3)
                acc = acc.reshape(CONF_ROWS, COL_GROUP)
                for v in range(COL_GROUP // LANES):
                    plane_scr[grp * (COL_GROUP // LANES) + v, pl.ds(q * t + r0, CONF_ROWS), :] = (
                        acc[:, v * LANES:(v + 1) * LANES])
            return 0

        lax.fori_loop(0, t // CONF_ROWS, conv_body, 0)

    o = _dot(ycat_scr[:, :LRU_WIDTH], w_out_ref[:LRU_WIDTH, :])
    for r0 in range(0, m, rc):
        gc = jnp.concatenate([plane_scr[p, r0:r0 + rc, :] for p in range(nplane)], axis=1)
        mu = jnp.mean(gc, axis=-1, keepdims=True)
        gcc = gc - mu
        var = jnp.mean(gcc * gcc, axis=-1, keepdims=True)
        yb = gcc * lax.rsqrt(var + EPS) * lng_ref[...] + lnb_ref[...]
        ycat_scr[r0:r0 + rc, LRU_WIDTH:] = (yb * _sigmoid(yb)).astype(_BF16)
    o = o + _dot(ycat_scr[:, LRU_WIDTH:], w_out_ref[LRU_WIDTH:, :])
    upd = g1 * _rms(o, ng_ref[1:2])
    for p in range(nplane):
        plane_scr[p] = upd[:, p * LANES:(p + 1) * LANES]

    def unpermute_body(i, _):
        for q in range(nseq):
            for s in range(SUBLANES):
                dst = pl.ds(pl.multiple_of(q * t + s * lc + i * SUBLANES, SUBLANES), SUBLANES)
                for p in range(nplane):
                    u = plane_scr[p, pl.ds(q * t + i * SUBLANES * SUBLANES + s, SUBLANES,
                                          stride=SUBLANES), :]
                    pc = slice(p * LANES, (p + 1) * LANES)
                    xo_ref[dst, pc] = xo_ref[dst, pc] + u
        return 0

    lax.fori_loop(0, lc // SUBLANES, unpermute_body, 0)


def _mixer_ab(x_all, pos, mod_tbl, norm_g, h0_all, w_in, cw, cb, wg, bg, lam, ccw, ccb, lng, lnb,
              w_out, *, layer, m, t, row0, nblk, mod_row0, mod_per_block, emit_state):
    rows, d = x_all.shape
    nseq = m // t
    blk0 = row0 // m
    j = layer // 2
    has_pos = pos is not None
    has_h0 = h0_all is not None
    args = [x_all]
    in_specs = [pl.BlockSpec((m, d), lambda i: (i + blk0, 0), pipeline_mode=pl.Buffered(1))]
    if has_pos:
        args.append(pos)
        in_specs.append(pl.BlockSpec(pos.shape, lambda i: (0, 0), pipeline_mode=pl.Buffered(1)))
    args.append(mod_tbl)
    in_specs.append(pl.BlockSpec(
        (None, None, N_MOD, d),
        (lambda i: (layer, i + mod_row0, 0, 0)) if mod_per_block
        else (lambda i: (layer, mod_row0, 0, 0))))
    args.append(norm_g)
    in_specs.append(_layer_spec(norm_g, layer))
    if has_h0:
        args.append(h0_all)
        in_specs.append(pl.BlockSpec((nseq, None, 2, LRU_WIDTH), lambda i: (i, j, 0, 0)))
    for w in (w_in, cw, cb, wg, bg, lam, ccw, ccb, lng, lnb, w_out):
        args.append(w)
        in_specs.append(_layer_spec(w, j))
    out_shape = [jax.ShapeDtypeStruct((rows, d), _F32)]
    out_specs = [pl.BlockSpec((m, d), lambda i: (i + blk0, 0), pipeline_mode=pl.Buffered(1))]
    if emit_state:
        out_shape.append(jax.ShapeDtypeStruct((nblk * nseq, 2, LRU_WIDTH), _F32))
        out_specs.append(pl.BlockSpec((nseq, 2, LRU_WIDTH), lambda i: (i, 0, 0)))
    scratch = [
        pltpu.VMEM((m, d), _BF16),
        pltpu.VMEM((m, LRU_WIDTH + CONF_WIDTH), _BF16),
        pltpu.VMEM((d // LANES, m, LANES), _F32),
    ] + [pltpu.VMEM((m, COL_GROUP), _F32)] * 7 + [
        pltpu.VMEM((nseq, t + 2 * LRU_LEFT * SUBLANES, COL_GROUP), _F32),
        pltpu.VMEM((nseq, t + 2 * CONF_HALF * SUBLANES, COL_GROUP), _F32),
        pltpu.VMEM((CONF_KERNEL, SUBLANES, COL_GROUP), _F32),
    ]
    res = pl.pallas_call(
        functools.partial(_ab_kernel, m=m, t=t, has_pos=has_pos, has_h0=has_h0,
                          emit_state=emit_state),
        grid=(nblk,),
        in_specs=in_specs,
        out_specs=out_specs,
        out_shape=out_shape,
        scratch_shapes=scratch,
        input_output_aliases={0: 0},
        compiler_params=pltpu.CompilerParams(
            dimension_semantics=("arbitrary",), vmem_limit_bytes=VMEM_LIMIT),
        name="mixer_ab",
    )(*args)
    return res if emit_state else (res[0], None)


def _c_kernel(x_ref, mod_ref, ng_ref, w_in_ref, lng_ref, lnb_ref, ws_ref, bs_ref, w_out_ref,
              xo_ref, h_scr, v_scr, y_scr, *, m):
    mod = mod_ref[...]
    sh1, sc1, g1 = mod[0:1], mod[1:2], mod[2:3]
    h_scr[...] = (_rms(x_ref[...], ng_ref[0:1]) * (1.0 + sc1) + sh1).astype(_BF16)
    h = h_scr[...]

    for hd in range(SGU_HEADS):
        c0 = SGU_WIDTH + hd * SGU_HEAD_DIM
        v_scr[:, hd * SGU_HEAD_DIM:(hd + 1) * SGU_HEAD_DIM] = _gelu(
            _dot(h, w_in_ref[:, c0:c0 + SGU_HEAD_DIM]))
    v = v_scr[...]
    mu = jnp.mean(v, axis=-1, keepdims=True)
    vc = v - mu
    var = jnp.mean(vc * vc, axis=-1, keepdims=True)
    rstd = lax.rsqrt(var + EPS)

    for hd in range(SGU_HEADS):
        cols = slice(hd * SGU_HEAD_DIM, (hd + 1) * SGU_HEAD_DIM)
        vn = ((v_scr[:, cols] - mu) * rstd * lng_ref[:, cols] + lnb_ref[:, cols]).astype(_BF16)
        u = _gelu(_dot(h, w_in_ref[:, cols]))
        w_s = ws_ref[hd]
        bias = bs_ref[:, hd:hd + 1]
        for n in range(m // CHUNK):
            rows = slice(n * CHUNK, (n + 1) * CHUNK)
            s = _dot(w_s, vn[rows]) + bias
            y_scr[rows, cols] = (u[rows] * s).astype(_BF16)

    o = _dot(y_scr[...], w_out_ref[...])
    xo_ref[...] = x_ref[...] + g1 * _rms(o, ng_ref[1:2])


def _mixer_c(x_all, mod_tbl, norm_g, w_in, lng, lnb, ws, bs_t, w_out, *, layer, m, mod_index):
    rows, d = x_all.shape
    j = layer // 2
    in_specs = [
        pl.BlockSpec((m, d), lambda i: (i, 0)),
        pl.BlockSpec((None, None, N_MOD, d), lambda i: (layer, mod_index(i), 0, 0)),
        _layer_spec(norm_g, layer),
    ] + [_layer_spec(w, j) for w in (w_in, lng, lnb, ws, bs_t, w_out)]
    return pl.pallas_call(
        functools.partial(_c_kernel, m=m),
        grid=(rows // m,),
        in_specs=in_specs,
        out_specs=pl.BlockSpec((m, d), lambda i: (i, 0)),
        out_shape=jax.ShapeDtypeStruct((rows, d), _F32),
        scratch_shapes=[
            pltpu.VMEM((m, d), _BF16),
            pltpu.VMEM((m, SGU_WIDTH), _F32),
            pltpu.VMEM((m, SGU_WIDTH), _BF16),
        ],
        compiler_params=pltpu.CompilerParams(
            dimension_semantics=("arbitrary",), vmem_limit_bytes=VMEM_LIMIT),
        name="mixer_c",
    )(x_all, mod_tbl, norm_g, w_in, lng, lnb, ws, bs_t, w_out)


def _ffn_kernel(x_ref, mod_ref, ng_ref, wup_ref, cw_ref, cb_ref, wd_ref, xo_ref, h_scr, act_scr,
                *, m, seg, n_short_blocks, t_long):
    long_seq = pl.program_id(0) >= n_short_blocks
    mod = mod_ref[...]
    sh2, sc2, g2 = mod[3:4], mod[4:5], mod[5:6]
    h_scr[...] = (_rms(x_ref[...], ng_ref[2:3]) * (1.0 + sc2) + sh2).astype(_BF16)
    h = h_scr[...]
    tn = COL_GROUP
    sub = lax.broadcasted_iota(jnp.int32, (SUBLANES, tn), 0)

    def conv(z, c0):
        outs = []
        for g in range(m // seg):
            zs = z[g * seg:(g + 1) * seg]
            prev = pltpu.roll(zs, 1, 0)
            nxt = pltpu.roll(zs, seg - 1, 0)
            if g > 0 and t_long > seg:
                before = jnp.where(long_seq, z[g * seg - 1:g * seg], 0.0)
            else:
                before = jnp.zeros((1, tn), _F32)
            if g < m // seg - 1 and t_long > seg:
                after = jnp.where(long_seq, z[(g + 1) * seg:(g + 1) * seg + 1], 0.0)
            else:
                after = jnp.zeros((1, tn), _F32)
            prev = jnp.concatenate(
                [jnp.where(sub == 0, before, prev[:SUBLANES]), prev[SUBLANES:]], axis=0)
            nxt = jnp.concatenate(
                [nxt[:seg - SUBLANES],
                 jnp.where(sub == SUBLANES - 1, after, nxt[seg - SUBLANES:])], axis=0)
            cols = slice(c0, c0 + tn)
            outs.append(cw_ref[0:1, cols] * prev + cw_ref[1:2, cols] * zs
                        + cw_ref[2:3, cols] * nxt + cb_ref[:, cols])
        return jnp.concatenate(outs, axis=0)

    for j in range(D_FF // tn):
        gate = conv(_dot(h, wup_ref[:, j * tn:(j + 1) * tn]), j * tn)
        val = conv(_dot(h, wup_ref[:, D_FF + j * tn:D_FF + (j + 1) * tn]), D_FF + j * tn)
        act_scr[:, j * tn:(j + 1) * tn] = (_gelu(gate) * val).astype(_BF16)

    o = _dot(act_scr[...], wd_ref[...])
    xo_ref[...] = x_ref[...] + g2 * _rms(o, ng_ref[3:4])


def _conv_ffn(x_all, mod_tbl, norm_g, w_up, conv_w, conv_b, w_down, *, layer, m, seg,
              n_short_blocks, t_long, mod_index):
    rows, d = x_all.shape
    return pl.pallas_call(
        functools.partial(_ffn_kernel, m=m, seg=seg, n_short_blocks=n_short_blocks,
                          t_long=t_long),
        grid=(rows // m,),
        in_specs=[
            pl.BlockSpec((m, d), lambda i: (i, 0)),
            pl.BlockSpec((None, None, N_MOD, d), lambda i: (layer, mod_index(i), 0, 0)),
        ] + [_layer_spec(w, layer) for w in (norm_g, w_up, conv_w, conv_b, w_down)],
        out_specs=pl.BlockSpec((m, d), lambda i: (i, 0)),
        out_shape=jax.ShapeDtypeStruct((rows, d), _F32),
        scratch_shapes=[pltpu.VMEM((m, d), _BF16), pltpu.VMEM((m, D_FF), _BF16)],
        compiler_params=pltpu.CompilerParams(
            dimension_semantics=("arbitrary",), vmem_limit_bytes=VMEM_LIMIT),
        name="conv_ffn",
    )(x_all, mod_tbl, norm_g, w_up, conv_w, conv_b, w_down)


def _grid_pos_embedding(t, d):
    rows = t // GRID_W
    row = jnp.repeat(jnp.arange(rows), GRID_W).astype(_F32)
    col = jnp.tile(jnp.arange(GRID_W), rows).astype(_F32)
    quarter = d // 4
    freq = jnp.exp(-math.log(POS_BASE) * jnp.arange(quarter, dtype=_F32) / quarter)

    def enc(p):
        ang = p[:, None] * freq[None, :]
        return jnp.concatenate([jnp.sin(ang), jnp.cos(ang)], axis=-1)

    return jnp.concatenate([enc(row), enc(col)], axis=-1)


def kernel(x_prompt, x_sample, state_lru, c, c_ctx, mod_w, mod_b, norm_g, ab_w_in, lru_conv_w, lru_conv_b, lru_w_gates, lru_b_gates, lru_lambda, conf_conv_w, conf_conv_b, conf_ln_g, conf_ln_b, ab_w_out, c_w_in, c_ln_g, c_ln_b, c_w_s, c_b_s, c_w_out, ffn_w_up, ffn_conv_w, ffn_conv_b, ffn_w_down):
    d = D_MODEL
    batch, seq, _ = x_prompt.shape
    dec_batch, dec_seq, _ = x_sample.shape
    n_ab = ab_w_in.shape[0]
    rows_p = batch * seq
    rows_s = dec_batch * dec_seq

    cond8 = jnp.concatenate(
        [c_ctx[None, :], c, jnp.zeros((MOD_ROWS - 1 - dec_batch, d), _F32)], axis=0)
    mod_tbl = jnp.transpose(_modulation(cond8, mod_w, mod_b), (0, 2, 1, 3))

    def mod_index(block_rows):
        first_latent = rows_p // block_rows
        per_seq = dec_seq // block_rows
        return lambda i: jnp.maximum((i - first_latent) // per_seq + 1, 0)

    w_ab = (
        ab_w_in.astype(_BF16), lru_conv_w, lru_conv_b[:, None, :],
        jnp.transpose(lru_w_gates, (0, 3, 4, 1, 2, 5)).reshape(
            n_ab, LRU_HEADS, LRU_BLOCK, 4 * LRU_BLOCK).astype(_BF16),
        lru_b_gates.reshape(n_ab, 4, LRU_WIDTH), lru_lambda,
        conf_conv_w, conf_conv_b[:, None, :], conf_ln_g[:, None, :], conf_ln_b[:, None, :],
        ab_w_out.astype(_BF16))
    w_c = (c_w_in.astype(_BF16), c_ln_g[:, None, :], c_ln_b[:, None, :], c_w_s.astype(_BF16),
           jnp.transpose(c_b_s, (0, 2, 1)), c_w_out.astype(_BF16))
    w_f = (ffn_w_up.astype(_BF16), ffn_conv_w, ffn_conv_b[:, None, :], ffn_w_down.astype(_BF16))

    pos = _grid_pos_embedding(dec_seq, d)
    x_all = jnp.concatenate([x_prompt.reshape(rows_p, d), x_sample.reshape(rows_s, d)], axis=0)
    m_p = 2 * seq
    m_s = dec_seq

    states = []
    for l in range(DEPTH):
        if l % 2 == 0:
            x_all, st = _mixer_ab(x_all, None, mod_tbl, norm_g, None, *w_ab, layer=l,
                                  m=m_p, t=seq, row0=0, nblk=rows_p // m_p, mod_row0=0,
                                  mod_per_block=False, emit_state=True)
            states.append(st)
            x_all, _ = _mixer_ab(x_all, pos if l == 0 else None, mod_tbl, norm_g, state_lru,
                                 *w_ab, layer=l, m=m_s, t=dec_seq, row0=rows_p, nblk=dec_batch,
                                 mod_row0=1, mod_per_block=True, emit_state=False)
        else:
            x_all = _mixer_c(x_all, mod_tbl, norm_g, *w_c, layer=l, m=C_ROWS,
                             mod_index=mod_index(C_ROWS))
        x_all = _conv_ffn(x_all, mod_tbl, norm_g, *w_f, layer=l, m=FFN_ROWS, seg=seq,
                          n_short_blocks=rows_p // FFN_ROWS, t_long=dec_seq,
                          mod_index=mod_index(FFN_ROWS))

    y_prompt = x_all[:rows_p].reshape(batch, seq, d)
    y_sample = x_all[rows_p:].reshape(dec_batch, dec_seq, d)
    new_state = jnp.stack(states, axis=1)
    return (y_prompt, y_sample, new_state)
```

```python
import functools
import math

import jax
import jax.numpy as jnp
from jax import lax
from jax.experimental import pallas as pl
from jax.experimental.pallas import tpu as pltpu

D_MODEL = 1024
DEPTH = 4
GRID_W = 64
LRU_WIDTH = 1024
LRU_HEADS = 8
LRU_BLOCK = LRU_WIDTH // LRU_HEADS
LRU_CONV = 4
LRU_C = 8.0
CONF_WIDTH = 1024
CONF_KERNEL = 31
CHUNK = 128
SGU_WIDTH = 2048
SGU_HEADS = 8
SGU_HEAD_DIM = SGU_WIDTH // SGU_HEADS
D_FF = 2816
FFN_CONV = 3
N_MOD = 6
EPS = 1e-6
POS_BASE = 10000.0

SUBLANES = 8
LANES = 128
BF16_ROWS = 16
COL_GROUP = 256
N_GROUPS = LRU_WIDTH // COL_GROUP
MOD_ROWS = 8
LRU_LEFT = LRU_CONV // 2
CONF_HALF = CONF_KERNEL // 2
CONF_ROWS = 32
ROW_CHUNK = 256
FFN_ROWS = 1024
C_ROWS = 512
VMEM_LIMIT = 56 * 1024 * 1024

_BF16 = jnp.bfloat16
_F32 = jnp.float32
_LOG2E = math.log2(math.e)
_GELU_K1 = -2.0 * math.sqrt(2.0 / math.pi) * _LOG2E
_GELU_K2 = _GELU_K1 * 0.044715


def _rms(x, g):
    return x * lax.rsqrt(jnp.mean(x * x, axis=-1, keepdims=True) + EPS) * g


def _dot(a, b):
    return jnp.dot(a, b, preferred_element_type=_F32)


def _sigmoid(x):
    return 1.0 / (1.0 + jnp.exp2(x * (-_LOG2E)))


def _gelu(x):
    return x / (1.0 + jnp.exp2(x * (_GELU_K1 + _GELU_K2 * (x * x))))


def _layer_spec(arr, layer):
    if layer is None:
        zeros = (0,) * arr.ndim
        return pl.BlockSpec(arr.shape, lambda *_: zeros, pipeline_mode=pl.Buffered(1))
    tail = (0,) * (arr.ndim - 1)
    return pl.BlockSpec((None,) + arr.shape[1:], lambda *_: (layer,) + tail,
                        pipeline_mode=pl.Buffered(1))


class _Casts:
    def __init__(self, items, ntiles, steps_per_tile):
        self.items = items
        self.ntiles = ntiles
        self.spt = steps_per_tile

    def __len__(self):
        return len(self.items)

    def args(self):
        return [arr for arr, _ in self.items]

    def in_specs(self):
        spt = self.spt
        return [pl.BlockSpec((None, arr.shape[1] // self.ntiles, arr.shape[2]),
                             functools.partial(lambda i, layer: (layer, i // spt, 0), layer=layer))
                for arr, layer in self.items]

    def out_specs(self):
        spt = self.spt
        return [pl.BlockSpec((arr.shape[1] // self.ntiles, arr.shape[2]), lambda i: (i // spt, 0))
                for arr, _ in self.items]

    def out_shapes(self):
        return [jax.ShapeDtypeStruct(arr.shape[1:], _BF16) for arr, _ in self.items]


def _run_casts(src_refs, dst_refs, steps_per_tile):
    def cast():
        for src, dst in zip(src_refs, dst_refs):
            dst[...] = src[...].astype(_BF16)

    if not src_refs:
        return
    if steps_per_tile == 1:
        cast()
    else:
        pl.when(pl.program_id(0) % steps_per_tile == 0)(cast)


def _mod_kernel(cond_ref, w_ref, b_ref, o_ref):
    c = cond_ref[...]
    s = (c * _sigmoid(c)).astype(_BF16)
    o_ref[0, 0] = _dot(s, w_ref[0].astype(_BF16)) + b_ref[0, 0]


def _modulation(cond8, mod_w, mod_b):
    d = D_MODEL
    return pl.pallas_call(
        _mod_kernel,
        grid=(DEPTH, N_MOD),
        in_specs=[
            pl.BlockSpec((MOD_ROWS, d), lambda l, n: (0, 0)),
            pl.BlockSpec((1, d, d), lambda l, n: (l, 0, n)),
            pl.BlockSpec((1, 1, 1, d), lambda l, n: (l, n, 0, 0)),
        ],
        out_specs=pl.BlockSpec((1, 1, MOD_ROWS, d), lambda l, n: (l, n, 0, 0)),
        out_shape=jax.ShapeDtypeStruct((DEPTH, N_MOD, MOD_ROWS, d), _F32),
        compiler_params=pltpu.CompilerParams(
            dimension_semantics=("arbitrary", "arbitrary"), vmem_limit_bytes=VMEM_LIMIT),
        name="modulation",
    )(cond8, mod_w, mod_b.reshape(DEPTH, N_MOD, 1, d))


def _ab_kernel(*refs, m, t, has_pos, has_h0, emit_state, n_cast):
    nseq = m // t
    lc = t // SUBLANES
    nplane = D_MODEL // LANES
    rc = min(ROW_CHUNK, t)
    refs = list(refs)
    x_ref = refs.pop(0)
    pos_ref = refs.pop(0) if has_pos else None
    mod_ref = refs.pop(0)
    ng_ref = refs.pop(0)
    h0_ref = refs.pop(0) if has_h0 else None
    (w_in_ref, cw_ref, cb_ref, wg_ref, bg_ref, lam_ref,
     ccw_ref, ccb_ref, lng_ref, lnb_ref, w_out_ref) = refs[:11]
    cast_src = refs[11:11 + n_cast]
    refs = refs[11 + n_cast:]
    xo_ref = refs.pop(0)
    st_ref = refs.pop(0) if emit_state else None
    cast_dst = refs[:n_cast]
    (h_scr, ycat_scr, plane_scr, af_scr, bf_scr, ab_scr, bb_scr, zg_scr, za_scr, zb_scr,
     xext_scr, gext_scr, wtap_scr) = refs[n_cast:]

    _run_casts(cast_src, cast_dst, 1)

    if has_pos:
        xo_ref[...] = x_ref[...] + pos_ref[...]
    else:
        xo_ref[...] = x_ref[...]
    mod = mod_ref[...]
    sh1, sc1, g1 = mod[0:1], mod[1:2], mod[2:3]

    for r0 in range(0, m, rc):
        hn = _rms(xo_ref[r0:r0 + rc, :], ng_ref[0:1]) * (1.0 + sc1) + sh1
        for p in range(nplane):
            plane_scr[p, r0:r0 + rc, :] = hn[:, p * LANES:(p + 1) * LANES]

    def permute_body(i, _):
        k = i * 2
        for q in range(nseq):
            dst = pl.multiple_of(q * t + k * SUBLANES, BF16_ROWS)
            for p in range(nplane):
                v0 = plane_scr[p, pl.ds(q * t + k, SUBLANES, stride=lc), :]
                v1 = plane_scr[p, pl.ds(q * t + k + 1, SUBLANES, stride=lc), :]
                h_scr[pl.ds(dst, BF16_ROWS), p * LANES:(p + 1) * LANES] = (
                    jnp.concatenate([v0, v1], axis=0).astype(_BF16))
        return 0

    lax.fori_loop(0, lc // 2, permute_body, 0)

    neg_lam = -lam_ref[...]
    softplus = jnp.maximum(neg_lam, 0.0) + jnp.log1p(jnp.exp(-jnp.abs(neg_lam)))
    decay = softplus * (-LRU_C * _LOG2E)

    sub = lax.broadcasted_iota(jnp.int32, (SUBLANES, COL_GROUP), 0)

    def front_halo(v):
        s_idx = lax.broadcasted_iota(jnp.int32, v.shape, 0) & (SUBLANES - 1)
        return jnp.where(s_idx == 0, 0.0, pltpu.roll(v, 1, 0))

    def back_halo(v):
        s_idx = lax.broadcasted_iota(jnp.int32, v.shape, 0) & (SUBLANES - 1)
        return jnp.where(s_idx == SUBLANES - 1, 0.0, pltpu.roll(v, v.shape[0] - 1, 0))

    def fill_halos(ext_scr, halo):
        hr = halo * SUBLANES
        for q in range(nseq):
            ext_scr[q, 0:hr, :] = front_halo(ext_scr[q, t:t + hr, :])
            ext_scr[q, hr + t:, :] = back_halo(ext_scr[q, hr:2 * hr, :])

    def project(dst_ref, row_slice, r0, col0):
        dst_ref[row_slice] = _dot(h_scr[r0:r0 + rc, :], w_in_ref[:, col0:col0 + COL_GROUP])

    def project_x(grp, r0):
        q, rq = divmod(r0, t)
        hr = LRU_LEFT * SUBLANES
        project(xext_scr, (q, slice(hr + rq, hr + rq + rc), slice(None)), r0,
                LRU_WIDTH + grp * COL_GROUP)

    for r0 in range(0, m, rc):
        project_x(0, r0)
    fill_halos(xext_scr, LRU_LEFT)

    for grp in range(N_GROUPS):
        c0 = grp * COL_GROUP
        cols = slice(c0, c0 + COL_GROUP)

        heads_per_group = COL_GROUP // LRU_BLOCK
        for r0 in range(0, m, rc):
            q, rq = divmod(r0, t)
            project(zg_scr, slice(r0, r0 + rc), r0, c0)
            project(za_scr, slice(r0, r0 + rc), r0, 2 * LRU_WIDTH + c0)
            project(zb_scr, slice(r0, r0 + rc), r0, 2 * LRU_WIDTH + CONF_WIDTH + c0)
            xc = jnp.broadcast_to(cb_ref[:, cols], (rc, COL_GROUP))
            for k in range(LRU_CONV):
                off = rq + k * SUBLANES
                xc = xc + cw_ref[k:k + 1, cols] * xext_scr[q, off:off + rc, :]
            xcb = xc.astype(_BF16)
            gl_parts = [
                _dot(xcb[:, hh * LRU_BLOCK:(hh + 1) * LRU_BLOCK],
                     wg_ref[grp * heads_per_group + hh])
                for hh in range(heads_per_group)]

            def gate(dg):
                parts = [gp[:, dg * LRU_BLOCK:(dg + 1) * LRU_BLOCK] for gp in gl_parts]
                return _sigmoid(jnp.concatenate(parts, axis=1) + bg_ref[dg:dg + 1, cols])

            for d, (a_scr, b_scr) in enumerate(((af_scr, bf_scr), (ab_scr, bb_scr))):
                r = gate(2 * d)
                i_gate = gate(2 * d + 1)
                a = jnp.exp2(r * decay[d:d + 1, cols])
                y = 1.0 - a * a
                a_scr[r0:r0 + rc, :] = a
                b_scr[r0:r0 + rc, :] = jnp.where(y > 0.0, y * lax.rsqrt(y), 0.0) * i_gate * xc

        def scan_body(i, carry):
            out = []
            for q in range(nseq):
                hf, df, hb, db = carry[q]
                rows_f = pl.ds(pl.multiple_of(q * t + i * SUBLANES, SUBLANES), SUBLANES)
                rows_b = pl.ds(pl.multiple_of(q * t + (lc - 1 - i) * SUBLANES, SUBLANES), SUBLANES)
                a = af_scr[rows_f, :]
                hf = a * hf + bf_scr[rows_f, :]
                df = df * a
                bf_scr[rows_f, :] = hf
                af_scr[rows_f, :] = df
                a = ab_scr[rows_b, :]
                hb = a * hb + bb_scr[rows_b, :]
                db = db * a
                bb_scr[rows_b, :] = hb
                ab_scr[rows_b, :] = db
                out.append((hf, df, hb, db))
            return tuple(out)

        zeros8 = jnp.zeros((SUBLANES, COL_GROUP), _F32)
        ones8 = jnp.ones((SUBLANES, COL_GROUP), _F32)
        ends = lax.fori_loop(0, lc, scan_body,
                             tuple((zeros8, ones8, zeros8, ones8) for _ in range(nseq)))

        entry = []
        for q in range(nseq):
            hf_end, df_end, hb_end, db_end = ends[q]
            if has_h0:
                cf = h0_ref[q, 0:1, cols]
                cb = h0_ref[q, 1:2, cols]
            else:
                cf = jnp.zeros((1, COL_GROUP), _F32)
                cb = cf
            cf_all = jnp.zeros((SUBLANES, COL_GROUP), _F32)
            for s in range(SUBLANES):
                cf_all = jnp.where(sub == s, cf, cf_all)
                cf = hf_end[s:s + 1] + df_end[s:s + 1] * cf
            cb_all = jnp.zeros((SUBLANES, COL_GROUP), _F32)
            for s in reversed(range(SUBLANES)):
                cb_all = jnp.where(sub == s, cb, cb_all)
                cb = hb_end[s:s + 1] + db_end[s:s + 1] * cb
            if emit_state:
                st_ref[q, 0:1, cols] = cf
                st_ref[q, 1:2, cols] = cb
            entry.append((cf_all, cb_all))
        gr = CONF_HALF * SUBLANES
        for r0 in range(0, m, rc):
            q, rq = divmod(r0, t)
            rows = slice(r0, r0 + rc)
            if grp + 1 < N_GROUPS:
                project_x(grp + 1, r0)
            cf_all, cb_all = entry[q]
            shape3 = (rc // SUBLANES, SUBLANES, COL_GROUP)
            hsum = (bf_scr[rows, :].reshape(shape3) + af_scr[rows, :].reshape(shape3) * cf_all[None]
                    + bb_scr[rows, :].reshape(shape3) + ab_scr[rows, :].reshape(shape3) * cb_all[None])
            ycat_scr[rows, cols] = (
                _gelu(zg_scr[rows, :]) * hsum.reshape(rc, COL_GROUP)).astype(_BF16)
            gext_scr[q, gr + rq:gr + rq + rc, :] = za_scr[rows, :] * _sigmoid(zb_scr[rows, :])
        fill_halos(gext_scr, CONF_HALF)
        if grp + 1 < N_GROUPS:
            fill_halos(xext_scr, LRU_LEFT)

        for k in range(CONF_KERNEL):
            wtap_scr[k] = jnp.broadcast_to(ccw_ref[k:k + 1, cols], (SUBLANES, COL_GROUP))
        shape3 = (CONF_ROWS // SUBLANES, SUBLANES, COL_GROUP)

        def conv_body(i, _):
            r0 = pl.multiple_of(i * CONF_ROWS, CONF_ROWS)
            for q in range(nseq):
                acc = jnp.broadcast_to(ccb_ref[:, cols], shape3)
                for k in range(CONF_KERNEL):
                    tile = gext_scr[q, pl.ds(r0 + k * SUBLANES, CONF_ROWS), :]
                    acc = acc + wtap_scr[k][None] * tile.reshape(shape3)
                acc = acc.reshape(CONF_ROWS, COL_GROUP)
                for v in range(COL_GROUP // LANES):
                    plane_scr[grp * (COL_GROUP // LANES) + v, pl.ds(q * t + r0, CONF_ROWS), :] = (
                        acc[:, v * LANES:(v + 1) * LANES])
            return 0

        lax.fori_loop(0, t // CONF_ROWS, conv_body, 0)

    for r0 in range(0, m, rc):
        rows = slice(r0, r0 + rc)
        gc = jnp.concatenate([plane_scr[p, rows, :] for p in range(nplane)], axis=1)
        mu = jnp.mean(gc, axis=-1, keepdims=True)
        gcc = gc - mu
        var = jnp.mean(gcc * gcc, axis=-1, keepdims=True)
        yb = gcc * lax.rsqrt(var + EPS) * lng_ref[...] + lnb_ref[...]
        ycat_scr[rows, LRU_WIDTH:] = (yb * _sigmoid(yb)).astype(_BF16)
        upd = g1 * _rms(_dot(ycat_scr[rows, :], w_out_ref[...]), ng_ref[1:2])
        for p in range(nplane):
            plane_scr[p, rows, :] = upd[:, p * LANES:(p + 1) * LANES]

    def unpermute_body(i, _):
        for q in range(nseq):
            for s in range(SUBLANES):
                dst = pl.ds(pl.multiple_of(q * t + s * lc + i * SUBLANES, SUBLANES), SUBLANES)
                for p in range(nplane):
                    u = plane_scr[p, pl.ds(q * t + i * SUBLANES * SUBLANES + s, SUBLANES,
                                          stride=SUBLANES), :]
                    pc = slice(p * LANES, (p + 1) * LANES)
                    xo_ref[dst, pc] = xo_ref[dst, pc] + u
        return 0

    lax.fori_loop(0, lc // SUBLANES, unpermute_body, 0)


def _mixer_ab(x_all, pos, mod_tbl, norm_g, h0_all, w_in, cw, cb, wg, bg, lam, ccw, ccb, lng, lnb,
              w_out, *, layer, m, t, row0, nblk, mod_row0, mod_per_block, emit_state, casts=None):
    rows, d = x_all.shape
    nseq = m // t
    blk0 = row0 // m
    j = layer // 2
    has_pos = pos is not None
    has_h0 = h0_all is not None
    n_cast = len(casts) if casts is not None else 0
    block_mode = dict(pipeline_mode=pl.Buffered(1)) if m * d * 4 > 2 * 1024 * 1024 else {}
    args = [x_all]
    in_specs = [pl.BlockSpec((m, d), lambda i: (i + blk0, 0), **block_mode)]
    if has_pos:
        args.append(pos)
        in_specs.append(pl.BlockSpec(pos.shape, lambda i: (0, 0), pipeline_mode=pl.Buffered(1)))
    args.append(mod_tbl)
    in_specs.append(pl.BlockSpec(
        (None, None, N_MOD, d),
        (lambda i: (layer, i + mod_row0, 0, 0)) if mod_per_block
        else (lambda i: (layer, mod_row0, 0, 0))))
    args.append(norm_g)
    in_specs.append(_layer_spec(norm_g, layer))
    if has_h0:
        args.append(h0_all)
        in_specs.append(pl.BlockSpec((nseq, None, 2, LRU_WIDTH), lambda i: (i, j, 0, 0)))
    for w in (w_in, cw, cb, wg, bg, lam, ccw, ccb, lng, lnb, w_out):
        args.append(w)
        in_specs.append(_layer_spec(w, None if w.ndim == 2 else j))
    out_shape = [jax.ShapeDtypeStruct((rows, d), _F32)]
    out_specs = [pl.BlockSpec((m, d), lambda i: (i + blk0, 0), **block_mode)]
    if emit_state:
        out_shape.append(jax.ShapeDtypeStruct((nblk * nseq, 2, LRU_WIDTH), _F32))
        out_specs.append(pl.BlockSpec((nseq, 2, LRU_WIDTH), lambda i: (i, 0, 0)))
    if n_cast:
        args += casts.args()
        in_specs += casts.in_specs()
        out_shape += casts.out_shapes()
        out_specs += casts.out_specs()
    scratch = [
        pltpu.VMEM((m, d), _BF16),
        pltpu.VMEM((m, LRU_WIDTH + CONF_WIDTH), _BF16),
        pltpu.VMEM((d // LANES, m, LANES), _F32),
    ] + [pltpu.VMEM((m, COL_GROUP), _F32)] * 7 + [
        pltpu.VMEM((nseq, t + 2 * LRU_LEFT * SUBLANES, COL_GROUP), _F32),
        pltpu.VMEM((nseq, t + 2 * CONF_HALF * SUBLANES, COL_GROUP), _F32),
        pltpu.VMEM((CONF_KERNEL, SUBLANES, COL_GROUP), _F32),
    ]
    res = pl.pallas_call(
        functools.partial(_ab_kernel, m=m, t=t, has_pos=has_pos, has_h0=has_h0,
                          emit_state=emit_state, n_cast=n_cast),
        grid=(nblk,),
        in_specs=in_specs,
        out_specs=out_specs,
        out_shape=out_shape,
        scratch_shapes=scratch,
        input_output_aliases={0: 0},
        compiler_params=pltpu.CompilerParams(
            dimension_semantics=("arbitrary",), vmem_limit_bytes=VMEM_LIMIT),
        name="mixer_ab",
    )(*args)
    n_main = 2 if emit_state else 1
    return res[0], (res[1] if emit_state else None), list(res[n_main:])


def _c_kernel(*refs, m, n_cast, steps_per_tile):
    (x_ref, mod_ref, ng_ref, w_in_ref, lng_ref, lnb_ref, ws_ref, bs_ref, w_out_ref) = refs[:9]
    cast_src = refs[9:9 + n_cast]
    xo_ref = refs[9 + n_cast]
    cast_dst = refs[10 + n_cast:10 + 2 * n_cast]
    h_scr, v_scr, y_scr = refs[10 + 2 * n_cast:]
    rc = min(ROW_CHUNK, m)

    _run_casts(cast_src, cast_dst, steps_per_tile)
    mod = mod_ref[...]
    sh1, sc1, g1 = mod[0:1], mod[1:2], mod[2:3]
    for r0 in range(0, m, rc):
        h_scr[r0:r0 + rc, :] = (
            _rms(x_ref[r0:r0 + rc, :], ng_ref[0:1]) * (1.0 + sc1) + sh1).astype(_BF16)
    h = h_scr[...]

    for hd in range(SGU_HEADS):
        c0 = SGU_WIDTH + hd * SGU_HEAD_DIM
        v_scr[:, hd * SGU_HEAD_DIM:(hd + 1) * SGU_HEAD_DIM] = _gelu(
            _dot(h, w_in_ref[:, c0:c0 + SGU_HEAD_DIM]))
    v = v_scr[...]
    mu = jnp.mean(v, axis=-1, keepdims=True)
    vc = v - mu
    var = jnp.mean(vc * vc, axis=-1, keepdims=True)
    rstd = lax.rsqrt(var + EPS)

    for hd in range(SGU_HEADS):
        cols = slice(hd * SGU_HEAD_DIM, (hd + 1) * SGU_HEAD_DIM)
        vn = ((v_scr[:, cols] - mu) * rstd * lng_ref[:, cols] + lnb_ref[:, cols]).astype(_BF16)
        u = _gelu(_dot(h, w_in_ref[:, cols]))
        w_s = ws_ref[hd]
        bias = bs_ref[:, hd:hd + 1]
        for n in range(m // CHUNK):
            rows = slice(n * CHUNK, (n + 1) * CHUNK)
            s = _dot(w_s, vn[rows]) + bias
            y_scr[rows, cols] = (u[rows] * s).astype(_BF16)

    for r0 in range(0, m, rc):
        rows = slice(r0, r0 + rc)
        o = _dot(y_scr[rows, :], w_out_ref[...])
        xo_ref[rows, :] = x_ref[rows, :] + g1 * _rms(o, ng_ref[1:2])


def _mixer_c(x_all, mod_tbl, norm_g, w_in, lng, lnb, ws, bs_t, w_out, *, layer, m, mod_index,
             casts=None):
    rows, d = x_all.shape
    j = layer // 2
    n_cast = len(casts) if casts is not None else 0
    args = [x_all, mod_tbl, norm_g, w_in, lng, lnb, ws, bs_t, w_out]
    in_specs = [
        pl.BlockSpec((m, d), lambda i: (i, 0)),
        pl.BlockSpec((None, None, N_MOD, d), lambda i: (layer, mod_index(i), 0, 0)),
        _layer_spec(norm_g, layer),
    ] + [_layer_spec(w, None if w.ndim == 2 else j) for w in (w_in, lng, lnb, ws, bs_t, w_out)]
    out_specs = [pl.BlockSpec((m, d), lambda i: (i, 0))]
    out_shape = [jax.ShapeDtypeStruct((rows, d), _F32)]
    if n_cast:
        args += casts.args()
        in_specs += casts.in_specs()
        out_shape += casts.out_shapes()
        out_specs += casts.out_specs()
    res = pl.pallas_call(
        functools.partial(_c_kernel, m=m, n_cast=n_cast,
                          steps_per_tile=casts.spt if n_cast else 1),
        grid=(rows // m,),
        in_specs=in_specs,
        out_specs=out_specs,
        out_shape=out_shape,
        scratch_shapes=[
            pltpu.VMEM((m, d), _BF16),
            pltpu.VMEM((m, SGU_WIDTH), _F32),
            pltpu.VMEM((m, SGU_WIDTH), _BF16),
        ],
        compiler_params=pltpu.CompilerParams(
            dimension_semantics=("arbitrary",), vmem_limit_bytes=VMEM_LIMIT),
        name="mixer_c",
    )(*args)
    return res[0], list(res[1:])


def _ffn_kernel(x_ref, mod_ref, ng_ref, wup_ref, cw_ref, cb_ref, wd_ref, xo_ref, h_scr, act_scr,
                *, m, seg, n_short_blocks, t_long):
    long_seq = pl.program_id(0) >= n_short_blocks
    mod = mod_ref[...]
    sh2, sc2, g2 = mod[3:4], mod[4:5], mod[5:6]
    for r0 in range(0, m, ROW_CHUNK):
        h_scr[r0:r0 + ROW_CHUNK, :] = (
            _rms(x_ref[r0:r0 + ROW_CHUNK, :], ng_ref[2:3]) * (1.0 + sc2) + sh2).astype(_BF16)
    h = h_scr[...]
    tn = COL_GROUP
    sub = lax.broadcasted_iota(jnp.int32, (SUBLANES, tn), 0)

    def conv(z, c0):
        outs = []
        for g in range(m // seg):
            zs = z[g * seg:(g + 1) * seg]
            prev = pltpu.roll(zs, 1, 0)
            nxt = pltpu.roll(zs, seg - 1, 0)
            if g > 0 and t_long > seg:
                before = jnp.where(long_seq, z[g * seg - 1:g * seg], 0.0)
            else:
                before = jnp.zeros((1, tn), _F32)
            if g < m // seg - 1 and t_long > seg:
                after = jnp.where(long_seq, z[(g + 1) * seg:(g + 1) * seg + 1], 0.0)
            else:
                after = jnp.zeros((1, tn), _F32)
            prev = jnp.concatenate(
                [jnp.where(sub == 0, before, prev[:SUBLANES]), prev[SUBLANES:]], axis=0)
            nxt = jnp.concatenate(
                [nxt[:seg - SUBLANES],
                 jnp.where(sub == SUBLANES - 1, after, nxt[seg - SUBLANES:])], axis=0)
            cols = slice(c0, c0 + tn)
            outs.append(cw_ref[0:1, cols] * prev + cw_ref[1:2, cols] * zs
                        + cw_ref[2:3, cols] * nxt + cb_ref[:, cols])
        return jnp.concatenate(outs, axis=0)

    for j in range(D_FF // tn):
        gate = conv(_dot(h, wup_ref[:, j * tn:(j + 1) * tn]), j * tn)
        val = conv(_dot(h, wup_ref[:, D_FF + j * tn:D_FF + (j + 1) * tn]), D_FF + j * tn)
        act_scr[:, j * tn:(j + 1) * tn] = (_gelu(gate) * val).astype(_BF16)

    for r0 in range(0, m, ROW_CHUNK):
        rows = slice(r0, r0 + ROW_CHUNK)
        o = _dot(act_scr[rows, :], wd_ref[...])
        xo_ref[rows, :] = x_ref[rows, :] + g2 * _rms(o, ng_ref[3:4])


def _conv_ffn(x_all, mod_tbl, norm_g, w_up, conv_w, conv_b, w_down, *, layer, m, seg,
              n_short_blocks, t_long, mod_index):
    rows, d = x_all.shape
    return pl.pallas_call(
        functools.partial(_ffn_kernel, m=m, seg=seg, n_short_blocks=n_short_blocks,
                          t_long=t_long),
        grid=(rows // m,),
        in_specs=[
            pl.BlockSpec((m, d), lambda i: (i, 0)),
            pl.BlockSpec((None, None, N_MOD, d), lambda i: (layer, mod_index(i), 0, 0)),
        ] + [_layer_spec(w, None if w.ndim == 2 else layer)
             for w in (norm_g, w_up, conv_w, conv_b, w_down)],
        out_specs=pl.BlockSpec((m, d), lambda i: (i, 0)),
        out_shape=jax.ShapeDtypeStruct((rows, d), _F32),
        scratch_shapes=[pltpu.VMEM((m, d), _BF16), pltpu.VMEM((m, D_FF), _BF16)],
        compiler_params=pltpu.CompilerParams(
            dimension_semantics=("arbitrary",), vmem_limit_bytes=VMEM_LIMIT),
        name="conv_ffn",
    )(x_all, mod_tbl, norm_g, w_up, conv_w, conv_b, w_down)


def _grid_pos_embedding(t, d):
    rows = t // GRID_W
    row = jnp.repeat(jnp.arange(rows), GRID_W).astype(_F32)
    col = jnp.tile(jnp.arange(GRID_W), rows).astype(_F32)
    quarter = d // 4
    freq = jnp.exp(-math.log(POS_BASE) * jnp.arange(quarter, dtype=_F32) / quarter)

    def enc(p):
        ang = p[:, None] * freq[None, :]
        return jnp.concatenate([jnp.sin(ang), jnp.cos(ang)], axis=-1)

    return jnp.concatenate([enc(row), enc(col)], axis=-1)


def kernel(x_prompt, x_sample, state_lru, c, c_ctx, mod_w, mod_b, norm_g, ab_w_in, lru_conv_w, lru_conv_b, lru_w_gates, lru_b_gates, lru_lambda, conf_conv_w, conf_conv_b, conf_ln_g, conf_ln_b, ab_w_out, c_w_in, c_ln_g, c_ln_b, c_w_s, c_b_s, c_w_out, ffn_w_up, ffn_conv_w, ffn_conv_b, ffn_w_down):
    d = D_MODEL
    batch, seq, _ = x_prompt.shape
    dec_batch, dec_seq, _ = x_sample.shape
    n_ab = ab_w_in.shape[0]
    rows_p = batch * seq
    rows_s = dec_batch * dec_seq

    cond8 = jnp.concatenate(
        [c_ctx[None, :], c, jnp.zeros((MOD_ROWS - 1 - dec_batch, d), _F32)], axis=0)
    mod_tbl = jnp.transpose(_modulation(cond8, mod_w, mod_b), (0, 2, 1, 3))

    def mod_index(block_rows):
        first_latent = rows_p // block_rows
        per_seq = dec_seq // block_rows
        return lambda i: jnp.maximum((i - first_latent) // per_seq + 1, 0)

    wg_b = jnp.transpose(lru_w_gates, (0, 3, 4, 1, 2, 5)).reshape(
        n_ab, LRU_HEADS, LRU_BLOCK, 4 * LRU_BLOCK).astype(_BF16)
    p_ab = (lru_conv_w, lru_conv_b[:, None, :], wg_b, lru_b_gates.reshape(n_ab, 4, LRU_WIDTH),
            lru_lambda, conf_conv_w, conf_conv_b[:, None, :], conf_ln_g[:, None, :],
            conf_ln_b[:, None, :])
    p_c = (c_ln_g[:, None, :], c_ln_b[:, None, :], c_w_s.astype(_BF16),
           jnp.transpose(c_b_s, (0, 2, 1)))

    pos = _grid_pos_embedding(dec_seq, d)
    x_all = jnp.concatenate([x_prompt.reshape(rows_p, d), x_sample.reshape(rows_s, d)], axis=0)
    m_p = 2 * seq
    m_s = dec_seq
    ab_blocks = rows_p // m_p
    c_blocks = (rows_p + rows_s) // C_ROWS
    c_tiles = 8

    mix_in, mix_out = ab_w_in[0].astype(_BF16), ab_w_out[0].astype(_BF16)
    states = []
    for l in range(DEPTH):
        j = l // 2
        items = [(ffn_w_up, l), (ffn_w_down, l)]
        if l % 2 == 0:
            items += [(c_w_in, j), (c_w_out, j)]
            x_all, st, cast = _mixer_ab(
                x_all, None, mod_tbl, norm_g, None, mix_in, *p_ab, mix_out, layer=l, m=m_p, t=seq,
                row0=0, nblk=ab_blocks, mod_row0=0, mod_per_block=False, emit_state=True,
                casts=_Casts(items, ab_blocks, 1))
            states.append(st)
            x_all, _, _ = _mixer_ab(
                x_all, pos if l == 0 else None, mod_tbl, norm_g, state_lru, mix_in, *p_ab, mix_out,
                layer=l, m=m_s, t=dec_seq, row0=rows_p, nblk=dec_batch, mod_row0=1,
                mod_per_block=True, emit_state=False)
        else:
            if l + 1 < DEPTH:
                items += [(ab_w_in, j + 1), (ab_w_out, j + 1)]
            x_all, cast = _mixer_c(
                x_all, mod_tbl, norm_g, mix_in, *p_c, mix_out, layer=l, m=C_ROWS,
                mod_index=mod_index(C_ROWS), casts=_Casts(items, c_tiles, c_blocks // c_tiles))
        w_up_b, w_down_b = cast[0], cast[1]
        if len(cast) > 2:
            mix_in, mix_out = cast[2], cast[3]
        x_all = _conv_ffn(x_all, mod_tbl, norm_g, w_up_b, ffn_conv_w, ffn_conv_b[:, None, :],
                          w_down_b, layer=l, m=FFN_ROWS, seg=seq,
                          n_short_blocks=rows_p // FFN_ROWS, t_long=dec_seq,
                          mod_index=mod_index(FFN_ROWS))

    y_prompt = x_all[:rows_p].reshape(batch, seq, d)
    y_sample = x_all[rows_p:].reshape(dec_batch, dec_seq, d)
    new_state = jnp.stack(states, axis=1)
    return (y_prompt, y_sample, new_state)
```

```python
import functools
import math

import jax
import jax.numpy as jnp
import numpy as np
from jax import lax
from jax.experimental import pallas as pl
from jax.experimental.pallas import tpu as pltpu

D_MODEL = 1024
DEPTH = 4
GRID_W = 64
LRU_WIDTH = 1024
LRU_HEADS = 8
LRU_BLOCK = LRU_WIDTH // LRU_HEADS
LRU_CONV = 4
LRU_C = 8.0
CONF_WIDTH = 1024
CONF_KERNEL = 31
CHUNK = 128
SGU_WIDTH = 2048
SGU_HEADS = 8
SGU_HEAD_DIM = SGU_WIDTH // SGU_HEADS
D_FF = 2816
FFN_CONV = 3
N_MOD = 6
EPS = 1e-6
POS_BASE = 10000.0

SUBLANES = 8
LANES = 128
BF16_ROWS = 16
COL_GROUP = 256
N_GROUPS = LRU_WIDTH // COL_GROUP
MOD_ROWS = 8
LRU_LEFT = LRU_CONV // 2
CONF_HALF = CONF_KERNEL // 2
CONF_ROWS = 64
ROW_CHUNK = 256
FFN_ROWS = 1024
C_ROWS = 512
VMEM_LIMIT = 56 * 1024 * 1024

_BF16 = jnp.bfloat16
_F32 = jnp.float32
_LOG2E = math.log2(math.e)
_GELU_K1 = -2.0 * math.sqrt(2.0 / math.pi) * _LOG2E
_GELU_K2 = _GELU_K1 * 0.044715


def _rms(x, g):
    return x * lax.rsqrt(jnp.mean(x * x, axis=-1, keepdims=True) + EPS) * g


def _dot(a, b):
    return jnp.dot(a, b, preferred_element_type=_F32)


def _sigmoid(x):
    return 1.0 / (1.0 + jnp.exp2(x * (-_LOG2E)))


def _gelu(x):
    return x / (1.0 + jnp.exp2(x * (_GELU_K1 + _GELU_K2 * (x * x))))


def _layer_spec(arr, layer):
    if layer is None:
        zeros = (0,) * arr.ndim
        return pl.BlockSpec(arr.shape, lambda *_: zeros, pipeline_mode=pl.Buffered(1))
    tail = (0,) * (arr.ndim - 1)
    return pl.BlockSpec((None,) + arr.shape[1:], lambda *_: (layer,) + tail,
                        pipeline_mode=pl.Buffered(1))


class _Casts:
    def __init__(self, items, ntiles, steps_per_tile):
        self.items = items
        self.ntiles = ntiles
        self.spt = steps_per_tile

    def __len__(self):
        return len(self.items)

    def args(self):
        return [arr for arr, _ in self.items]

    def in_specs(self):
        spt = self.spt
        return [pl.BlockSpec((None, arr.shape[1] // self.ntiles, arr.shape[2]),
                             functools.partial(lambda i, layer: (layer, i // spt, 0), layer=layer))
                for arr, layer in self.items]

    def out_specs(self):
        spt = self.spt
        return [pl.BlockSpec((arr.shape[1] // self.ntiles, arr.shape[2]), lambda i: (i // spt, 0))
                for arr, _ in self.items]

    def out_shapes(self):
        return [jax.ShapeDtypeStruct(arr.shape[1:], _BF16) for arr, _ in self.items]


def _run_casts(src_refs, dst_refs, steps_per_tile):
    def cast():
        for src, dst in zip(src_refs, dst_refs):
            dst[...] = src[...].astype(_BF16)

    if not src_refs:
        return
    if steps_per_tile == 1:
        cast()
    else:
        pl.when(pl.program_id(0) % steps_per_tile == 0)(cast)


def _mod_kernel(cond_ref, w_ref, b_ref, o_ref):
    c = cond_ref[...]
    s = (c * _sigmoid(c)).astype(_BF16)
    o_ref[0, 0] = _dot(s, w_ref[0].astype(_BF16)) + b_ref[0, 0]


def _modulation(cond8, mod_w, mod_b):
    d = D_MODEL
    return pl.pallas_call(
        _mod_kernel,
        grid=(DEPTH, N_MOD),
        in_specs=[
            pl.BlockSpec((MOD_ROWS, d), lambda l, n: (0, 0)),
            pl.BlockSpec((1, d, d), lambda l, n: (l, 0, n)),
            pl.BlockSpec((1, 1, 1, d), lambda l, n: (l, n, 0, 0)),
        ],
        out_specs=pl.BlockSpec((1, 1, MOD_ROWS, d), lambda l, n: (l, n, 0, 0)),
        out_shape=jax.ShapeDtypeStruct((DEPTH, N_MOD, MOD_ROWS, d), _F32),
        compiler_params=pltpu.CompilerParams(
            dimension_semantics=("arbitrary", "arbitrary"), vmem_limit_bytes=VMEM_LIMIT),
        name="modulation",
    )(cond8, mod_w, mod_b.reshape(DEPTH, N_MOD, 1, d))


def _ab_kernel(*refs, m, t, has_pos, has_perm, has_h0, emit_state, n_cast):
    nseq = m // t
    lc = t // SUBLANES
    nplane = D_MODEL // LANES
    rc = min(ROW_CHUNK, t)
    conv_rows = CONF_ROWS // nseq
    refs = list(refs)
    x_ref = refs.pop(0)
    pos_ref = refs.pop(0) if has_pos else None
    perm_ref = refs.pop(0) if has_perm else None
    mod_ref = refs.pop(0)
    ng_ref = refs.pop(0)
    h0_ref = refs.pop(0) if has_h0 else None
    (w_in_ref, cw_ref, cb_ref, wg_ref, bg_ref, lam_ref,
     ccw_ref, ccb_ref, lng_ref, lnb_ref, w_out_ref) = refs[:11]
    cast_src = refs[11:11 + n_cast]
    refs = refs[11 + n_cast:]
    xo_ref = refs.pop(0)
    st_ref = refs.pop(0) if emit_state else None
    cast_dst = refs[:n_cast]
    (h_scr, ycat_scr, plane_scr, af_scr, bf_scr, ab_scr, bb_scr, zg_scr, za_scr, zb_scr,
     xext_scr, gext_scr, wtap_scr) = refs[n_cast:]

    _run_casts(cast_src, cast_dst, 1)

    def residual(rows, cols=slice(None)):
        if has_pos:
            return x_ref[rows, cols] + pos_ref[rows, cols]
        return x_ref[rows, cols]

    mod = mod_ref[...]
    sh1, sc1, g1 = mod[0:1], mod[1:2], mod[2:3]

    for r0 in range(0, m, rc):
        hn = _rms(residual(slice(r0, r0 + rc)), ng_ref[0:1]) * (1.0 + sc1) + sh1
        if has_perm:
            h_scr[r0:r0 + rc, :] = _dot(perm_ref[...], hn.astype(_BF16)).astype(_BF16)
        else:
            for p in range(nplane):
                plane_scr[p, r0:r0 + rc, :] = hn[:, p * LANES:(p + 1) * LANES]

    def permute_body(i, _):
        k = i * 2
        for q in range(nseq):
            dst = pl.multiple_of(q * t + k * SUBLANES, BF16_ROWS)
            for p in range(nplane):
                v0 = plane_scr[p, pl.ds(q * t + k, SUBLANES, stride=lc), :]
                v1 = plane_scr[p, pl.ds(q * t + k + 1, SUBLANES, stride=lc), :]
                h_scr[pl.ds(dst, BF16_ROWS), p * LANES:(p + 1) * LANES] = (
                    jnp.concatenate([v0, v1], axis=0).astype(_BF16))
        return 0

    if not has_perm:
        lax.fori_loop(0, lc // 2, permute_body, 0)

    neg_lam = -lam_ref[...]
    softplus = jnp.maximum(neg_lam, 0.0) + jnp.log1p(jnp.exp(-jnp.abs(neg_lam)))
    decay = softplus * (-LRU_C * _LOG2E)

    sub = lax.broadcasted_iota(jnp.int32, (SUBLANES, COL_GROUP), 0)

    def front_halo(v):
        s_idx = lax.broadcasted_iota(jnp.int32, v.shape, 0) & (SUBLANES - 1)
        return jnp.where(s_idx == 0, 0.0, pltpu.roll(v, 1, 0))

    def back_halo(v):
        s_idx = lax.broadcasted_iota(jnp.int32, v.shape, 0) & (SUBLANES - 1)
        return jnp.where(s_idx == SUBLANES - 1, 0.0, pltpu.roll(v, v.shape[0] - 1, 0))

    def fill_halos(ext_scr, halo):
        hr = halo * SUBLANES
        for q in range(nseq):
            ext_scr[q, 0:hr, :] = front_halo(ext_scr[q, t:t + hr, :])
            ext_scr[q, hr + t:, :] = back_halo(ext_scr[q, hr:2 * hr, :])

    def project(dst_ref, row_slice, r0, col0):
        dst_ref[row_slice] = _dot(h_scr[r0:r0 + rc, :], w_in_ref[:, col0:col0 + COL_GROUP])

    def project_x(grp, r0):
        q, rq = divmod(r0, t)
        hr = LRU_LEFT * SUBLANES
        project(xext_scr, (q, slice(hr + rq, hr + rq + rc), slice(None)), r0,
                LRU_WIDTH + grp * COL_GROUP)

    for r0 in range(0, m, rc):
        project_x(0, r0)
    fill_halos(xext_scr, LRU_LEFT)

    for grp in range(N_GROUPS):
        c0 = grp * COL_GROUP
        cols = slice(c0, c0 + COL_GROUP)

        heads_per_group = COL_GROUP // LRU_BLOCK
        for r0 in range(0, m, rc):
            q, rq = divmod(r0, t)
            project(zg_scr, slice(r0, r0 + rc), r0, c0)
            project(za_scr, slice(r0, r0 + rc), r0, 2 * LRU_WIDTH + c0)
            project(zb_scr, slice(r0, r0 + rc), r0, 2 * LRU_WIDTH + CONF_WIDTH + c0)
            xc = jnp.broadcast_to(cb_ref[:, cols], (rc, COL_GROUP))
            for k in range(LRU_CONV):
                off = rq + k * SUBLANES
                xc = xc + cw_ref[k:k + 1, cols] * xext_scr[q, off:off + rc, :]
            xcb = xc.astype(_BF16)
            gl_parts = [
                _dot(xcb[:, hh * LRU_BLOCK:(hh + 1) * LRU_BLOCK],
                     wg_ref[grp * heads_per_group + hh])
                for hh in range(heads_per_group)]

            def gate(dg):
                parts = [gp[:, dg * LRU_BLOCK:(dg + 1) * LRU_BLOCK] for gp in gl_parts]
                return _sigmoid(jnp.concatenate(parts, axis=1) + bg_ref[dg:dg + 1, cols])

            for d, (a_scr, b_scr) in enumerate(((af_scr, bf_scr), (ab_scr, bb_scr))):
                r = gate(2 * d)
                i_gate = gate(2 * d + 1)
                a = jnp.exp2(r * decay[d:d + 1, cols])
                y = 1.0 - a * a
                a_scr[r0:r0 + rc, :] = a
                b_scr[r0:r0 + rc, :] = jnp.where(y > 0.0, y * lax.rsqrt(y), 0.0) * i_gate * xc

        def scan_body(i, carry):
            out = []
            for q in range(nseq):
                hf, df, hb, db = carry[q]
                rows_f = pl.ds(pl.multiple_of(q * t + i * SUBLANES, SUBLANES), SUBLANES)
                rows_b = pl.ds(pl.multiple_of(q * t + (lc - 1 - i) * SUBLANES, SUBLANES), SUBLANES)
                a = af_scr[rows_f, :]
                hf = a * hf + bf_scr[rows_f, :]
                df = df * a
                bf_scr[rows_f, :] = hf
                af_scr[rows_f, :] = df
                a = ab_scr[rows_b, :]
                hb = a * hb + bb_scr[rows_b, :]
                db = db * a
                bb_scr[rows_b, :] = hb
                ab_scr[rows_b, :] = db
                out.append((hf, df, hb, db))
            return tuple(out)

        zeros8 = jnp.zeros((SUBLANES, COL_GROUP), _F32)
        ones8 = jnp.ones((SUBLANES, COL_GROUP), _F32)
        ends = lax.fori_loop(0, lc, scan_body,
                             tuple((zeros8, ones8, zeros8, ones8) for _ in range(nseq)))

        entry = []
        for q in range(nseq):
            hf_end, df_end, hb_end, db_end = ends[q]
            if has_h0:
                cf = h0_ref[q, 0:1, cols]
                cb = h0_ref[q, 1:2, cols]
            else:
                cf = jnp.zeros((1, COL_GROUP), _F32)
                cb = cf
            cf_all = jnp.zeros((SUBLANES, COL_GROUP), _F32)
            for s in range(SUBLANES):
                cf_all = jnp.where(sub == s, cf, cf_all)
                cf = hf_end[s:s + 1] + df_end[s:s + 1] * cf
            cb_all = jnp.zeros((SUBLANES, COL_GROUP), _F32)
            for s in reversed(range(SUBLANES)):
                cb_all = jnp.where(sub == s, cb, cb_all)
                cb = hb_end[s:s + 1] + db_end[s:s + 1] * cb
            if emit_state:
                st_ref[q, 0:1, cols] = cf
                st_ref[q, 1:2, cols] = cb
            entry.append((cf_all, cb_all))
        gr = CONF_HALF * SUBLANES
        for r0 in range(0, m, rc):
            q, rq = divmod(r0, t)
            rows = slice(r0, r0 + rc)
            if grp + 1 < N_GROUPS:
                project_x(grp + 1, r0)
            cf_all, cb_all = entry[q]
            shape3 = (rc // SUBLANES, SUBLANES, COL_GROUP)
            hsum = (bf_scr[rows, :].reshape(shape3) + af_scr[rows, :].reshape(shape3) * cf_all[None]
                    + bb_scr[rows, :].reshape(shape3) + ab_scr[rows, :].reshape(shape3) * cb_all[None])
            ycat_scr[rows, cols] = (
                _gelu(zg_scr[rows, :]) * hsum.reshape(rc, COL_GROUP)).astype(_BF16)
            gext_scr[q, gr + rq:gr + rq + rc, :] = za_scr[rows, :] * _sigmoid(zb_scr[rows, :])
        fill_halos(gext_scr, CONF_HALF)
        if grp + 1 < N_GROUPS:
            fill_halos(xext_scr, LRU_LEFT)

        for k in range(CONF_KERNEL):
            wtap_scr[k] = jnp.broadcast_to(ccw_ref[k:k + 1, cols], (SUBLANES, COL_GROUP))
        shape3 = (conv_rows // SUBLANES, SUBLANES, COL_GROUP)

        def conv_body(i, _):
            r0 = pl.multiple_of(i * conv_rows, conv_rows)
            for q in range(nseq):
                acc = jnp.broadcast_to(ccb_ref[:, cols], shape3)
                for k in range(CONF_KERNEL):
                    tile = gext_scr[q, pl.ds(r0 + k * SUBLANES, conv_rows), :]
                    acc = acc + wtap_scr[k][None] * tile.reshape(shape3)
                acc = acc.reshape(conv_rows, COL_GROUP)
                for v in range(COL_GROUP // LANES):
                    plane_scr[grp * (COL_GROUP // LANES) + v, pl.ds(q * t + r0, conv_rows), :] = (
                        acc[:, v * LANES:(v + 1) * LANES])
            return 0

        lax.fori_loop(0, t // conv_rows, conv_body, 0)

    for r0 in range(0, m, rc):
        rows = slice(r0, r0 + rc)
        gc = jnp.concatenate([plane_scr[p, rows, :] for p in range(nplane)], axis=1)
        mu = jnp.mean(gc, axis=-1, keepdims=True)
        gcc = gc - mu
        var = jnp.mean(gcc * gcc, axis=-1, keepdims=True)
        yb = gcc * lax.rsqrt(var + EPS) * lng_ref[...] + lnb_ref[...]
        ycat_scr[rows, LRU_WIDTH:] = (yb * _sigmoid(yb)).astype(_BF16)
        upd = g1 * _rms(_dot(ycat_scr[rows, :], w_out_ref[...]), ng_ref[1:2])
        for p in range(nplane):
            plane_scr[p, rows, :] = upd[:, p * LANES:(p + 1) * LANES]

    def unpermute_body(i, _):
        for q in range(nseq):
            for s in range(SUBLANES):
                dst = pl.ds(pl.multiple_of(q * t + s * lc + i * SUBLANES, SUBLANES), SUBLANES)
                for p in range(nplane):
                    u = plane_scr[p, pl.ds(q * t + i * SUBLANES * SUBLANES + s, SUBLANES,
                                          stride=SUBLANES), :]
                    pc = slice(p * LANES, (p + 1) * LANES)
                    xo_ref[dst, pc] = residual(dst, pc) + u
        return 0

    lax.fori_loop(0, lc // SUBLANES, unpermute_body, 0)


def _mixer_ab(x_all, pos, mod_tbl, norm_g, h0_all, w_in, cw, cb, wg, bg, lam, ccw, ccb, lng, lnb,
              w_out, *, layer, m, t, row0, nblk, mod_row0, mod_per_block, emit_state, casts=None):
    rows, d = x_all.shape
    nseq = m // t
    blk0 = row0 // m
    j = layer // 2
    has_pos = pos is not None
    has_h0 = h0_all is not None
    n_cast = len(casts) if casts is not None else 0
    single = dict(pipeline_mode=pl.Buffered(1))
    big = m * d * 4 > 2 * 1024 * 1024
    args = [x_all]
    in_specs = [pl.BlockSpec((m, d), lambda i: (i + blk0, 0), **(single if big and has_pos else {}))]
    if has_pos:
        args.append(pos)
        in_specs.append(pl.BlockSpec(pos.shape, lambda i: (0, 0), **single))
    has_perm = t <= ROW_CHUNK
    if has_perm:
        idx = np.arange(t)
        perm = np.zeros((t, t), np.float32)
        perm[idx, (idx % SUBLANES) * (t // SUBLANES) + idx // SUBLANES] = 1.0
        args.append(jnp.asarray(perm, _BF16))
        in_specs.append(pl.BlockSpec((t, t), lambda i: (0, 0), **single))
    args.append(mod_tbl)
    in_specs.append(pl.BlockSpec(
        (None, None, N_MOD, d),
        (lambda i: (layer, i + mod_row0, 0, 0)) if mod_per_block
        else (lambda i: (layer, mod_row0, 0, 0))))
    args.append(norm_g)
    in_specs.append(_layer_spec(norm_g, layer))
    if has_h0:
        args.append(h0_all)
        in_specs.append(pl.BlockSpec((nseq, None, 2, LRU_WIDTH), lambda i: (i, j, 0, 0)))
    for w in (w_in, cw, cb, wg, bg, lam, ccw, ccb, lng, lnb, w_out):
        args.append(w)
        in_specs.append(_layer_spec(w, None if w.ndim == 2 else j))
    out_shape = [jax.ShapeDtypeStruct((rows, d), _F32)]
    out_specs = [pl.BlockSpec((m, d), lambda i: (i + blk0, 0), **(single if big else {}))]
    if emit_state:
        out_shape.append(jax.ShapeDtypeStruct((nblk * nseq, 2, LRU_WIDTH), _F32))
        out_specs.append(pl.BlockSpec((nseq, 2, LRU_WIDTH), lambda i: (i, 0, 0)))
    if n_cast:
        args += casts.args()
        in_specs += casts.in_specs()
        out_shape += casts.out_shapes()
        out_specs += casts.out_specs()
    scratch = [
        pltpu.VMEM((m, d), _BF16),
        pltpu.VMEM((m, LRU_WIDTH + CONF_WIDTH), _BF16),
        pltpu.VMEM((d // LANES, m, LANES), _F32),
    ] + [pltpu.VMEM((m, COL_GROUP), _F32)] * 7 + [
        pltpu.VMEM((nseq, t + 2 * LRU_LEFT * SUBLANES, COL_GROUP), _F32),
        pltpu.VMEM((nseq, t + 2 * CONF_HALF * SUBLANES, COL_GROUP), _F32),
        pltpu.VMEM((CONF_KERNEL, SUBLANES, COL_GROUP), _F32),
    ]
    res = pl.pallas_call(
        functools.partial(_ab_kernel, m=m, t=t, has_pos=has_pos, has_perm=has_perm,
                          has_h0=has_h0, emit_state=emit_state, n_cast=n_cast),
        grid=(nblk,),
        in_specs=in_specs,
        out_specs=out_specs,
        out_shape=out_shape,
        scratch_shapes=scratch,
        input_output_aliases={0: 0},
        compiler_params=pltpu.CompilerParams(
            dimension_semantics=("arbitrary",), vmem_limit_bytes=VMEM_LIMIT),
        name="mixer_ab",
    )(*args)
    n_main = 2 if emit_state else 1
    return res[0], (res[1] if emit_state else None), list(res[n_main:])


def _c_kernel(*refs, m, n_cast, steps_per_tile):
    (x_ref, mod_ref, ng_ref, w_in_ref, lng_ref, lnb_ref, ws_ref, bs_ref, w_out_ref) = refs[:9]
    cast_src = refs[9:9 + n_cast]
    xo_ref = refs[9 + n_cast]
    cast_dst = refs[10 + n_cast:10 + 2 * n_cast]
    h_scr, v_scr, y_scr = refs[10 + 2 * n_cast:]
    rc = min(ROW_CHUNK, m)

    _run_casts(cast_src, cast_dst, steps_per_tile)
    mod = mod_ref[...]
    sh1, sc1, g1 = mod[0:1], mod[1:2], mod[2:3]
    for r0 in range(0, m, rc):
        h_scr[r0:r0 + rc, :] = (
            _rms(x_ref[r0:r0 + rc, :], ng_ref[0:1]) * (1.0 + sc1) + sh1).astype(_BF16)
    h = h_scr[...]

    for hd in range(SGU_HEADS):
        c0 = SGU_WIDTH + hd * SGU_HEAD_DIM
        v_scr[:, hd * SGU_HEAD_DIM:(hd + 1) * SGU_HEAD_DIM] = _gelu(
            _dot(h, w_in_ref[:, c0:c0 + SGU_HEAD_DIM]))
    v = v_scr[...]
    mu = jnp.mean(v, axis=-1, keepdims=True)
    vc = v - mu
    var = jnp.mean(vc * vc, axis=-1, keepdims=True)
    rstd = lax.rsqrt(var + EPS)

    for hd in range(SGU_HEADS):
        cols = slice(hd * SGU_HEAD_DIM, (hd + 1) * SGU_HEAD_DIM)
        vn = ((v_scr[:, cols] - mu) * rstd * lng_ref[:, cols] + lnb_ref[:, cols]).astype(_BF16)
        u = _gelu(_dot(h, w_in_ref[:, cols]))
        w_s = ws_ref[hd]
        bias = bs_ref[:, hd:hd + 1]
        for n in range(m // CHUNK):
            rows = slice(n * CHUNK, (n + 1) * CHUNK)
            s = _dot(w_s, vn[rows]) + bias
            y_scr[rows, cols] = (u[rows] * s).astype(_BF16)

    for r0 in range(0, m, rc):
        rows = slice(r0, r0 + rc)
        o = _dot(y_scr[rows, :], w_out_ref[...])
        xo_ref[rows, :] = x_ref[rows, :] + g1 * _rms(o, ng_ref[1:2])


def _mixer_c(x_all, mod_tbl, norm_g, w_in, lng, lnb, ws, bs_t, w_out, *, layer, m, mod_index,
             casts=None):
    rows, d = x_all.shape
    j = layer // 2
    n_cast = len(casts) if casts is not None else 0
    args = [x_all, mod_tbl, norm_g, w_in, lng, lnb, ws, bs_t, w_out]
    in_specs = [
        pl.BlockSpec((m, d), lambda i: (i, 0)),
        pl.BlockSpec((None, None, N_MOD, d), lambda i: (layer, mod_index(i), 0, 0)),
        _layer_spec(norm_g, layer),
    ] + [_layer_spec(w, None if w.ndim == 2 else j) for w in (w_in, lng, lnb, ws, bs_t, w_out)]
    out_specs = [pl.BlockSpec((m, d), lambda i: (i, 0))]
    out_shape = [jax.ShapeDtypeStruct((rows, d), _F32)]
    if n_cast:
        args += casts.args()
        in_specs += casts.in_specs()
        out_shape += casts.out_shapes()
        out_specs += casts.out_specs()
    res = pl.pallas_call(
        functools.partial(_c_kernel, m=m, n_cast=n_cast,
                          steps_per_tile=casts.spt if n_cast else 1),
        grid=(rows // m,),
        in_specs=in_specs,
        out_specs=out_specs,
        out_shape=out_shape,
        scratch_shapes=[
            pltpu.VMEM((m, d), _BF16),
            pltpu.VMEM((m, SGU_WIDTH), _F32),
            pltpu.VMEM((m, SGU_WIDTH), _BF16),
        ],
        compiler_params=pltpu.CompilerParams(
            dimension_semantics=("arbitrary",), vmem_limit_bytes=VMEM_LIMIT),
        name="mixer_c",
    )(*args)
    return res[0], list(res[1:])


def _ffn_kernel(x_ref, mod_ref, ng_ref, wup_ref, cw_ref, cb_ref, wd_ref, xo_ref, h_scr, act_scr,
                *, m, seg, n_short_blocks, t_long, blk0):
    long_seq = pl.program_id(0) + blk0 >= n_short_blocks
    mod = mod_ref[...]
    sh2, sc2, g2 = mod[3:4], mod[4:5], mod[5:6]
    for r0 in range(0, m, ROW_CHUNK):
        h_scr[r0:r0 + ROW_CHUNK, :] = (
            _rms(x_ref[r0:r0 + ROW_CHUNK, :], ng_ref[2:3]) * (1.0 + sc2) + sh2).astype(_BF16)
    h = h_scr[...]
    tn = COL_GROUP
    sub = lax.broadcasted_iota(jnp.int32, (SUBLANES, tn), 0)

    def conv(z, c0):
        outs = []
        for g in range(m // seg):
            zs = z[g * seg:(g + 1) * seg]
            prev = pltpu.roll(zs, 1, 0)
            nxt = pltpu.roll(zs, seg - 1, 0)
            if g > 0 and t_long > seg:
                before = jnp.where(long_seq, z[g * seg - 1:g * seg], 0.0)
            else:
                before = jnp.zeros((1, tn), _F32)
            if g < m // seg - 1 and t_long > seg:
                after = jnp.where(long_seq, z[(g + 1) * seg:(g + 1) * seg + 1], 0.0)
            else:
                after = jnp.zeros((1, tn), _F32)
            prev = jnp.concatenate(
                [jnp.where(sub == 0, before, prev[:SUBLANES]), prev[SUBLANES:]], axis=0)
            nxt = jnp.concatenate(
                [nxt[:seg - SUBLANES],
                 jnp.where(sub == SUBLANES - 1, after, nxt[seg - SUBLANES:])], axis=0)
            cols = slice(c0, c0 + tn)
            outs.append(cw_ref[0:1, cols] * prev + cw_ref[1:2, cols] * zs
                        + cw_ref[2:3, cols] * nxt + cb_ref[:, cols])
        return jnp.concatenate(outs, axis=0)

    for j in range(D_FF // tn):
        gate = conv(_dot(h, wup_ref[:, j * tn:(j + 1) * tn]), j * tn)
        val = conv(_dot(h, wup_ref[:, D_FF + j * tn:D_FF + (j + 1) * tn]), D_FF + j * tn)
        act_scr[:, j * tn:(j + 1) * tn] = (_gelu(gate) * val).astype(_BF16)

    for r0 in range(0, m, ROW_CHUNK):
        rows = slice(r0, r0 + ROW_CHUNK)
        o = _dot(act_scr[rows, :], wd_ref[...])
        xo_ref[rows, :] = x_ref[rows, :] + g2 * _rms(o, ng_ref[3:4])


def _conv_ffn(x_all, mod_tbl, norm_g, w_up, conv_w, conv_b, w_down, *, layer, m, seg,
              n_short_blocks, t_long, mod_index, blk0=0, nblk=None):
    rows, d = x_all.shape
    if nblk is None:
        nblk = rows // m
    return pl.pallas_call(
        functools.partial(_ffn_kernel, m=m, seg=seg, n_short_blocks=n_short_blocks,
                          t_long=t_long, blk0=blk0),
        grid=(nblk,),
        in_specs=[
            pl.BlockSpec((m, d), lambda i: (i + blk0, 0)),
            pl.BlockSpec((None, None, N_MOD, d), lambda i: (layer, mod_index(i + blk0), 0, 0)),
        ] + [_layer_spec(w, None if w.ndim == 2 else layer)
             for w in (norm_g, w_up, conv_w, conv_b, w_down)],
        out_specs=pl.BlockSpec((m, d), lambda i: (i, 0)),
        out_shape=jax.ShapeDtypeStruct((nblk * m, d), _F32),
        scratch_shapes=[pltpu.VMEM((m, d), _BF16), pltpu.VMEM((m, D_FF), _BF16)],
        compiler_params=pltpu.CompilerParams(
            dimension_semantics=("arbitrary",), vmem_limit_bytes=VMEM_LIMIT),
        name="conv_ffn",
    )(x_all, mod_tbl, norm_g, w_up, conv_w, conv_b, w_down)


def _grid_pos_embedding(t, d):
    rows = t // GRID_W
    row = jnp.repeat(jnp.arange(rows), GRID_W).astype(_F32)
    col = jnp.tile(jnp.arange(GRID_W), rows).astype(_F32)
    quarter = d // 4
    freq = jnp.exp(-math.log(POS_BASE) * jnp.arange(quarter, dtype=_F32) / quarter)

    def enc(p):
        ang = p[:, None] * freq[None, :]
        return jnp.concatenate([jnp.sin(ang), jnp.cos(ang)], axis=-1)

    return jnp.concatenate([enc(row), enc(col)], axis=-1)


def kernel(x_prompt, x_sample, state_lru, c, c_ctx, mod_w, mod_b, norm_g, ab_w_in, lru_conv_w, lru_conv_b, lru_w_gates, lru_b_gates, lru_lambda, conf_conv_w, conf_conv_b, conf_ln_g, conf_ln_b, ab_w_out, c_w_in, c_ln_g, c_ln_b, c_w_s, c_b_s, c_w_out, ffn_w_up, ffn_conv_w, ffn_conv_b, ffn_w_down):
    d = D_MODEL
    batch, seq, _ = x_prompt.shape
    dec_batch, dec_seq, _ = x_sample.shape
    n_ab = ab_w_in.shape[0]
    rows_p = batch * seq
    rows_s = dec_batch * dec_seq

    cond8 = jnp.concatenate(
        [c_ctx[None, :], c, jnp.zeros((MOD_ROWS - 1 - dec_batch, d), _F32)], axis=0)
    mod_tbl = jnp.transpose(_modulation(cond8, mod_w, mod_b), (0, 2, 1, 3))

    def mod_index(block_rows):
        first_latent = rows_p // block_rows
        per_seq = dec_seq // block_rows
        return lambda i: jnp.maximum((i - first_latent) // per_seq + 1, 0)

    wg_b = jnp.transpose(lru_w_gates, (0, 3, 4, 1, 2, 5)).reshape(
        n_ab, LRU_HEADS, LRU_BLOCK, 4 * LRU_BLOCK).astype(_BF16)
    p_ab = (lru_conv_w, lru_conv_b[:, None, :], wg_b, lru_b_gates.reshape(n_ab, 4, LRU_WIDTH),
            lru_lambda, conf_conv_w, conf_conv_b[:, None, :], conf_ln_g[:, None, :],
            conf_ln_b[:, None, :])
    p_c = (c_ln_g[:, None, :], c_ln_b[:, None, :], c_w_s.astype(_BF16),
           jnp.transpose(c_b_s, (0, 2, 1)))

    pos = _grid_pos_embedding(dec_seq, d)
    x_all = jnp.concatenate([x_prompt.reshape(rows_p, d), x_sample.reshape(rows_s, d)], axis=0)
    m_p = 2 * seq
    m_s = dec_seq
    ab_blocks = rows_p // m_p
    c_blocks = (rows_p + rows_s) // C_ROWS
    c_tiles = 8

    mix_in, mix_out = ab_w_in[0].astype(_BF16), ab_w_out[0].astype(_BF16)
    states = []
    for l in range(DEPTH):
        j = l // 2
        items = [(ffn_w_up, l), (ffn_w_down, l)]
        if l % 2 == 0:
            items += [(c_w_in, j), (c_w_out, j)]
            x_all, st, cast = _mixer_ab(
                x_all, None, mod_tbl, norm_g, None, mix_in, *p_ab, mix_out, layer=l, m=m_p, t=seq,
                row0=0, nblk=ab_blocks, mod_row0=0, mod_per_block=False, emit_state=True,
                casts=_Casts(items, ab_blocks, 1))
            states.append(st)
            x_all, _, _ = _mixer_ab(
                x_all, pos if l == 0 else None, mod_tbl, norm_g, state_lru, mix_in, *p_ab, mix_out,
                layer=l, m=m_s, t=dec_seq, row0=rows_p, nblk=dec_batch, mod_row0=1,
                mod_per_block=True, emit_state=False)
        else:
            if l + 1 < DEPTH:
                items += [(ab_w_in, j + 1), (ab_w_out, j + 1)]
            x_all, cast = _mixer_c(
                x_all, mod_tbl, norm_g, mix_in, *p_c, mix_out, layer=l, m=C_ROWS,
                mod_index=mod_index(C_ROWS), casts=_Casts(items, c_tiles, c_blocks // c_tiles))
        w_up_b, w_down_b = cast[0], cast[1]
        if len(cast) > 2:
            mix_in, mix_out = cast[2], cast[3]
        ffn = functools.partial(
            _conv_ffn, x_all, mod_tbl, norm_g, w_up_b, ffn_conv_w, ffn_conv_b[:, None, :],
            w_down_b, layer=l, m=FFN_ROWS, seg=seq, n_short_blocks=rows_p // FFN_ROWS,
            t_long=dec_seq, mod_index=mod_index(FFN_ROWS))
        if l + 1 < DEPTH:
            x_all = ffn()
        else:
            y_p = ffn(blk0=0, nblk=rows_p // FFN_ROWS)
            y_s = ffn(blk0=rows_p // FFN_ROWS, nblk=rows_s // FFN_ROWS)

    y_prompt = y_p.reshape(batch, seq, d)
    y_sample = y_s.reshape(dec_batch, dec_seq, d)
    new_state = jnp.stack(states, axis=1)
    return (y_prompt, y_sample, new_state)
```

```python
import functools
import math

import jax
import jax.numpy as jnp
import numpy as np
from jax import lax
from jax.experimental import pallas as pl
from jax.experimental.pallas import tpu as pltpu

D_MODEL = 1024
DEPTH = 4
GRID_W = 64
LRU_WIDTH = 1024
LRU_HEADS = 8
LRU_BLOCK = LRU_WIDTH // LRU_HEADS
LRU_CONV = 4
LRU_C = 8.0
CONF_WIDTH = 1024
CONF_KERNEL = 31
CHUNK = 128
SGU_WIDTH = 2048
SGU_HEADS = 8
SGU_HEAD_DIM = SGU_WIDTH // SGU_HEADS
D_FF = 2816
FFN_CONV = 3
N_MOD = 6
EPS = 1e-6
POS_BASE = 10000.0

SUBLANES = 8
LANES = 128
BF16_ROWS = 16
COL_GROUP = 256
N_GROUPS = LRU_WIDTH // COL_GROUP
MOD_ROWS = 8
LRU_LEFT = LRU_CONV // 2
CONF_HALF = CONF_KERNEL // 2
CONF_ROWS = 64
ROW_CHUNK = 256
SCAN_UNROLL = 8
LOOP_UNROLL = 4
FFN_ROWS = 1024
C_ROWS = 512
VMEM_LIMIT = 56 * 1024 * 1024

_BF16 = jnp.bfloat16
_F32 = jnp.float32
_LOG2E = math.log2(math.e)
_GELU_K1 = -2.0 * math.sqrt(2.0 / math.pi) * _LOG2E
_GELU_K2 = _GELU_K1 * 0.044715


def _rms(x, g):
    return x * lax.rsqrt(jnp.mean(x * x, axis=-1, keepdims=True) + EPS) * g


def _dot(a, b):
    return jnp.dot(a, b, preferred_element_type=_F32)


def _sigmoid(x):
    return 1.0 / (1.0 + jnp.exp2(x * (-_LOG2E)))


def _gelu(x):
    return x / (1.0 + jnp.exp2(x * (_GELU_K1 + _GELU_K2 * (x * x))))


def _layer_spec(arr, layer):
    if layer is None:
        zeros = (0,) * arr.ndim
        return pl.BlockSpec(arr.shape, lambda *_: zeros, pipeline_mode=pl.Buffered(1))
    tail = (0,) * (arr.ndim - 1)
    return pl.BlockSpec((None,) + arr.shape[1:], lambda *_: (layer,) + tail,
                        pipeline_mode=pl.Buffered(1))


class _Casts:
    def __init__(self, items, ntiles, steps_per_tile):
        self.items = items
        self.ntiles = ntiles
        self.spt = steps_per_tile

    def __len__(self):
        return len(self.items)

    def args(self):
        return [arr for arr, _ in self.items]

    def in_specs(self):
        spt = self.spt
        return [pl.BlockSpec((None, arr.shape[1] // self.ntiles, arr.shape[2]),
                             functools.partial(lambda i, layer: (layer, i // spt, 0), layer=layer))
                for arr, layer in self.items]

    def out_specs(self):
        spt = self.spt
        return [pl.BlockSpec((arr.shape[1] // self.ntiles, arr.shape[2]), lambda i: (i // spt, 0))
                for arr, _ in self.items]

    def out_shapes(self):
        return [jax.ShapeDtypeStruct(arr.shape[1:], _BF16) for arr, _ in self.items]


def _run_casts(src_refs, dst_refs, steps_per_tile):
    def cast():
        for src, dst in zip(src_refs, dst_refs):
            dst[...] = src[...].astype(_BF16)

    if not src_refs:
        return
    if steps_per_tile == 1:
        cast()
    else:
        pl.when(pl.program_id(0) % steps_per_tile == 0)(cast)


def _mod_kernel(cond_ref, w_ref, b_ref, o_ref):
    c = cond_ref[...]
    s = (c * _sigmoid(c)).astype(_BF16)
    o_ref[0, 0] = _dot(s, w_ref[0].astype(_BF16)) + b_ref[0, 0]


def _modulation(cond8, mod_w, mod_b):
    d = D_MODEL
    return pl.pallas_call(
        _mod_kernel,
        grid=(DEPTH, N_MOD),
        in_specs=[
            pl.BlockSpec((MOD_ROWS, d), lambda l, n: (0, 0)),
            pl.BlockSpec((1, d, d), lambda l, n: (l, 0, n)),
            pl.BlockSpec((1, 1, 1, d), lambda l, n: (l, n, 0, 0)),
        ],
        out_specs=pl.BlockSpec((1, 1, MOD_ROWS, d), lambda l, n: (l, n, 0, 0)),
        out_shape=jax.ShapeDtypeStruct((DEPTH, N_MOD, MOD_ROWS, d), _F32),
        compiler_params=pltpu.CompilerParams(
            dimension_semantics=("arbitrary", "arbitrary"), vmem_limit_bytes=VMEM_LIMIT),
        name="modulation",
    )(cond8, mod_w, mod_b.reshape(DEPTH, N_MOD, 1, d))


def _ab_kernel(*refs, m, t, has_pos, has_perm, has_h0, emit_state, n_cast):
    nseq = m // t
    lc = t // SUBLANES
    nplane = D_MODEL // LANES
    rc = min(ROW_CHUNK, t)
    conv_rows = CONF_ROWS // nseq
    refs = list(refs)
    x_ref = refs.pop(0)
    pos_ref = refs.pop(0) if has_pos else None
    perm_ref = refs.pop(0) if has_perm else None
    mod_ref = refs.pop(0)
    ng_ref = refs.pop(0)
    h0_ref = refs.pop(0) if has_h0 else None
    (w_in_ref, cw_ref, cb_ref, wg_ref, bg_ref, lam_ref,
     ccw_ref, ccb_ref, lng_ref, lnb_ref, w_out_ref) = refs[:11]
    cast_src = refs[11:11 + n_cast]
    refs = refs[11 + n_cast:]
    xo_ref = refs.pop(0)
    st_ref = refs.pop(0) if emit_state else None
    cast_dst = refs[:n_cast]
    (h_scr, ycat_scr, plane_scr, af_scr, bf_scr, ab_scr, bb_scr, zg_scr, za_scr, zb_scr,
     xext_scr, gext_scr, wtap_scr) = refs[n_cast:]

    _run_casts(cast_src, cast_dst, 1)

    def residual(rows, cols=slice(None)):
        if has_pos:
            return x_ref[rows, cols] + pos_ref[rows, cols]
        return x_ref[rows, cols]

    mod = mod_ref[...]
    sh1, sc1, g1 = mod[0:1], mod[1:2], mod[2:3]

    for r0 in range(0, m, rc):
        hn = _rms(residual(slice(r0, r0 + rc)), ng_ref[0:1]) * (1.0 + sc1) + sh1
        if has_perm:
            h_scr[r0:r0 + rc, :] = _dot(perm_ref[...], hn.astype(_BF16)).astype(_BF16)
        else:
            for p in range(nplane):
                plane_scr[p, r0:r0 + rc, :] = hn[:, p * LANES:(p + 1) * LANES]

    def permute_body(i, _):
        k = i * 2
        for q in range(nseq):
            dst = pl.multiple_of(q * t + k * SUBLANES, BF16_ROWS)
            for p in range(nplane):
                v0 = plane_scr[p, pl.ds(q * t + k, SUBLANES, stride=lc), :]
                v1 = plane_scr[p, pl.ds(q * t + k + 1, SUBLANES, stride=lc), :]
                h_scr[pl.ds(dst, BF16_ROWS), p * LANES:(p + 1) * LANES] = (
                    jnp.concatenate([v0, v1], axis=0).astype(_BF16))
        return 0

    if not has_perm:
        lax.fori_loop(0, lc // 2, permute_body, 0, unroll=LOOP_UNROLL)

    neg_lam = -lam_ref[...]
    softplus = jnp.maximum(neg_lam, 0.0) + jnp.log1p(jnp.exp(-jnp.abs(neg_lam)))
    decay = softplus * (-LRU_C * _LOG2E)

    sub = lax.broadcasted_iota(jnp.int32, (SUBLANES, COL_GROUP), 0)

    def front_halo(v):
        s_idx = lax.broadcasted_iota(jnp.int32, v.shape, 0) & (SUBLANES - 1)
        return jnp.where(s_idx == 0, 0.0, pltpu.roll(v, 1, 0))

    def back_halo(v):
        s_idx = lax.broadcasted_iota(jnp.int32, v.shape, 0) & (SUBLANES - 1)
        return jnp.where(s_idx == SUBLANES - 1, 0.0, pltpu.roll(v, v.shape[0] - 1, 0))

    def fill_halos(ext_scr, halo):
        hr = halo * SUBLANES
        for q in range(nseq):
            ext_scr[q, 0:hr, :] = front_halo(ext_scr[q, t:t + hr, :])
            ext_scr[q, hr + t:, :] = back_halo(ext_scr[q, hr:2 * hr, :])

    def project(dst_ref, row_slice, r0, col0):
        dst_ref[row_slice] = _dot(h_scr[r0:r0 + rc, :], w_in_ref[:, col0:col0 + COL_GROUP])

    def project_x(grp, r0):
        q, rq = divmod(r0, t)
        hr = LRU_LEFT * SUBLANES
        project(xext_scr, (q, slice(hr + rq, hr + rq + rc), slice(None)), r0,
                LRU_WIDTH + grp * COL_GROUP)

    for r0 in range(0, m, rc):
        project_x(0, r0)
    fill_halos(xext_scr, LRU_LEFT)

    for grp in range(N_GROUPS):
        c0 = grp * COL_GROUP
        cols = slice(c0, c0 + COL_GROUP)

        heads_per_group = COL_GROUP // LRU_BLOCK
        for r0 in range(0, m, rc):
            q, rq = divmod(r0, t)
            project(zg_scr, slice(r0, r0 + rc), r0, c0)
            project(za_scr, slice(r0, r0 + rc), r0, 2 * LRU_WIDTH + c0)
            project(zb_scr, slice(r0, r0 + rc), r0, 2 * LRU_WIDTH + CONF_WIDTH + c0)
            xc = jnp.broadcast_to(cb_ref[:, cols], (rc, COL_GROUP))
            for k in range(LRU_CONV):
                off = rq + k * SUBLANES
                xc = xc + cw_ref[k:k + 1, cols] * xext_scr[q, off:off + rc, :]
            xcb = xc.astype(_BF16)
            gl_parts = [
                _dot(xcb[:, hh * LRU_BLOCK:(hh + 1) * LRU_BLOCK],
                     wg_ref[grp * heads_per_group + hh])
                for hh in range(heads_per_group)]

            def gate(dg):
                parts = [gp[:, dg * LRU_BLOCK:(dg + 1) * LRU_BLOCK] for gp in gl_parts]
                return _sigmoid(jnp.concatenate(parts, axis=1) + bg_ref[dg:dg + 1, cols])

            for d, (a_scr, b_scr) in enumerate(((af_scr, bf_scr), (ab_scr, bb_scr))):
                r = gate(2 * d)
                i_gate = gate(2 * d + 1)
                a = jnp.exp2(r * decay[d:d + 1, cols])
                y = 1.0 - a * a
                a_scr[r0:r0 + rc, :] = a
                b_scr[r0:r0 + rc, :] = jnp.where(y > 0.0, y * lax.rsqrt(y), 0.0) * i_gate * xc

        def scan_body(i, carry):
            out = []
            for q in range(nseq):
                hf, df, hb, db = carry[q]
                rows_f = pl.ds(pl.multiple_of(q * t + i * SUBLANES, SUBLANES), SUBLANES)
                rows_b = pl.ds(pl.multiple_of(q * t + (lc - 1 - i) * SUBLANES, SUBLANES), SUBLANES)
                a = af_scr[rows_f, :]
                hf = a * hf + bf_scr[rows_f, :]
                df = df * a
                bf_scr[rows_f, :] = hf
                af_scr[rows_f, :] = df
                a = ab_scr[rows_b, :]
                hb = a * hb + bb_scr[rows_b, :]
                db = db * a
                bb_scr[rows_b, :] = hb
                ab_scr[rows_b, :] = db
                out.append((hf, df, hb, db))
            return tuple(out)

        zeros8 = jnp.zeros((SUBLANES, COL_GROUP), _F32)
        ones8 = jnp.ones((SUBLANES, COL_GROUP), _F32)
        ends = lax.fori_loop(0, lc, scan_body,
                             tuple((zeros8, ones8, zeros8, ones8) for _ in range(nseq)),
                             unroll=SCAN_UNROLL)

        entry = []
        for q in range(nseq):
            hf_end, df_end, hb_end, db_end = ends[q]
            if has_h0:
                cf = h0_ref[q, 0:1, cols]
                cb = h0_ref[q, 1:2, cols]
            else:
                cf = jnp.zeros((1, COL_GROUP), _F32)
                cb = cf
            cf_all = jnp.zeros((SUBLANES, COL_GROUP), _F32)
            for s in range(SUBLANES):
                cf_all = jnp.where(sub == s, cf, cf_all)
                cf = hf_end[s:s + 1] + df_end[s:s + 1] * cf
            cb_all = jnp.zeros((SUBLANES, COL_GROUP), _F32)
            for s in reversed(range(SUBLANES)):
                cb_all = jnp.where(sub == s, cb, cb_all)
                cb = hb_end[s:s + 1] + db_end[s:s + 1] * cb
            if emit_state:
                st_ref[q, 0:1, cols] = cf
                st_ref[q, 1:2, cols] = cb
            entry.append((cf_all, cb_all))
        gr = CONF_HALF * SUBLANES
        for r0 in range(0, m, rc):
            q, rq = divmod(r0, t)
            rows = slice(r0, r0 + rc)
            if grp + 1 < N_GROUPS:
                project_x(grp + 1, r0)
            cf_all, cb_all = entry[q]
            shape3 = (rc // SUBLANES, SUBLANES, COL_GROUP)
            hsum = (bf_scr[rows, :].reshape(shape3) + af_scr[rows, :].reshape(shape3) * cf_all[None]
                    + bb_scr[rows, :].reshape(shape3) + ab_scr[rows, :].reshape(shape3) * cb_all[None])
            ycat_scr[rows, cols] = (
                _gelu(zg_scr[rows, :]) * hsum.reshape(rc, COL_GROUP)).astype(_BF16)
            gext_scr[q, gr + rq:gr + rq + rc, :] = za_scr[rows, :] * _sigmoid(zb_scr[rows, :])
        fill_halos(gext_scr, CONF_HALF)
        if grp + 1 < N_GROUPS:
            fill_halos(xext_scr, LRU_LEFT)

        for k in range(CONF_KERNEL):
            wtap_scr[k] = jnp.broadcast_to(ccw_ref[k:k + 1, cols], (SUBLANES, COL_GROUP))
        shape3 = (conv_rows // SUBLANES, SUBLANES, COL_GROUP)

        def conv_body(i, _):
            r0 = pl.multiple_of(i * conv_rows, conv_rows)
            for q in range(nseq):
                acc = jnp.broadcast_to(ccb_ref[:, cols], shape3)
                for k in range(CONF_KERNEL):
                    tile = gext_scr[q, pl.ds(r0 + k * SUBLANES, conv_rows), :]
                    acc = acc + wtap_scr[k][None] * tile.reshape(shape3)
                acc = acc.reshape(conv_rows, COL_GROUP)
                for v in range(COL_GROUP // LANES):
                    plane_scr[grp * (COL_GROUP // LANES) + v, pl.ds(q * t + r0, conv_rows), :] = (
                        acc[:, v * LANES:(v + 1) * LANES])
            return 0

        lax.fori_loop(0, t // conv_rows, conv_body, 0, unroll=2)

    for r0 in range(0, m, rc):
        rows = slice(r0, r0 + rc)
        gc = jnp.concatenate([plane_scr[p, rows, :] for p in range(nplane)], axis=1)
        mu = jnp.mean(gc, axis=-1, keepdims=True)
        gcc = gc - mu
        var = jnp.mean(gcc * gcc, axis=-1, keepdims=True)
        yb = gcc * lax.rsqrt(var + EPS) * lng_ref[...] + lnb_ref[...]
        ycat_scr[rows, LRU_WIDTH:] = (yb * _sigmoid(yb)).astype(_BF16)
        upd = g1 * _rms(_dot(ycat_scr[rows, :], w_out_ref[...]), ng_ref[1:2])
        for p in range(nplane):
            plane_scr[p, rows, :] = upd[:, p * LANES:(p + 1) * LANES]

    def unpermute_body(i, _):
        for q in range(nseq):
            for s in range(SUBLANES):
                dst = pl.ds(pl.multiple_of(q * t + s * lc + i * SUBLANES, SUBLANES), SUBLANES)
                for p in range(nplane):
                    u = plane_scr[p, pl.ds(q * t + i * SUBLANES * SUBLANES + s, SUBLANES,
                                          stride=SUBLANES), :]
                    pc = slice(p * LANES, (p + 1) * LANES)
                    xo_ref[dst, pc] = residual(dst, pc) + u
        return 0

    lax.fori_loop(0, lc // SUBLANES, unpermute_body, 0, unroll=LOOP_UNROLL)


def _mixer_ab(x_all, pos, mod_tbl, norm_g, h0_all, w_in, cw, cb, wg, bg, lam, ccw, ccb, lng, lnb,
              w_out, *, layer, m, t, row0, nblk, mod_row0, mod_per_block, emit_state, casts=None):
    rows, d = x_all.shape
    nseq = m // t
    blk0 = row0 // m
    j = layer // 2
    has_pos = pos is not None
    has_h0 = h0_all is not None
    n_cast = len(casts) if casts is not None else 0
    single = dict(pipeline_mode=pl.Buffered(1))
    big = m * d * 4 > 2 * 1024 * 1024
    args = [x_all]
    in_specs = [pl.BlockSpec((m, d), lambda i: (i + blk0, 0), **(single if big and has_pos else {}))]
    if has_pos:
        args.append(pos)
        in_specs.append(pl.BlockSpec(pos.shape, lambda i: (0, 0), **single))
    has_perm = t <= ROW_CHUNK
    if has_perm:
        idx = np.arange(t)
        perm = np.zeros((t, t), np.float32)
        perm[idx, (idx % SUBLANES) * (t // SUBLANES) + idx // SUBLANES] = 1.0
        args.append(jnp.asarray(perm, _BF16))
        in_specs.append(pl.BlockSpec((t, t), lambda i: (0, 0), **single))
    args.append(mod_tbl)
    in_specs.append(pl.BlockSpec(
        (None, None, N_MOD, d),
        (lambda i: (layer, i + mod_row0, 0, 0)) if mod_per_block
        else (lambda i: (layer, mod_row0, 0, 0))))
    args.append(norm_g)
    in_specs.append(_layer_spec(norm_g, layer))
    if has_h0:
        args.append(h0_all)
        in_specs.append(pl.BlockSpec((nseq, None, 2, LRU_WIDTH), lambda i: (i, j, 0, 0)))
    for w in (w_in, cw, cb, wg, bg, lam, ccw, ccb, lng, lnb, w_out):
        args.append(w)
        in_specs.append(_layer_spec(w, None if w.ndim == 2 else j))
    out_shape = [jax.ShapeDtypeStruct((rows, d), _F32)]
    out_specs = [pl.BlockSpec((m, d), lambda i: (i + blk0, 0), **(single if big else {}))]
    if emit_state:
        out_shape.append(jax.ShapeDtypeStruct((nblk * nseq, 2, LRU_WIDTH), _F32))
        out_specs.append(pl.BlockSpec((nseq, 2, LRU_WIDTH), lambda i: (i, 0, 0)))
    if n_cast:
        args += casts.args()
        in_specs += casts.in_specs()
        out_shape += casts.out_shapes()
        out_specs += casts.out_specs()
    scratch = [
        pltpu.VMEM((m, d), _BF16),
        pltpu.VMEM((m, LRU_WIDTH + CONF_WIDTH), _BF16),
        pltpu.VMEM((d // LANES, m, LANES), _F32),
    ] + [pltpu.VMEM((m, COL_GROUP), _F32)] * 7 + [
        pltpu.VMEM((nseq, t + 2 * LRU_LEFT * SUBLANES, COL_GROUP), _F32),
        pltpu.VMEM((nseq, t + 2 * CONF_HALF * SUBLANES, COL_GROUP), _F32),
        pltpu.VMEM((CONF_KERNEL, SUBLANES, COL_GROUP), _F32),
    ]
    res = pl.pallas_call(
        functools.partial(_ab_kernel, m=m, t=t, has_pos=has_pos, has_perm=has_perm,
                          has_h0=has_h0, emit_state=emit_state, n_cast=n_cast),
        grid=(nblk,),
        in_specs=in_specs,
        out_specs=out_specs,
        out_shape=out_shape,
        scratch_shapes=scratch,
        input_output_aliases={0: 0},
        compiler_params=pltpu.CompilerParams(
            dimension_semantics=("arbitrary",), vmem_limit_bytes=VMEM_LIMIT),
        name="mixer_ab",
    )(*args)
    n_main = 2 if emit_state else 1
    return res[0], (res[1] if emit_state else None), list(res[n_main:])


def _c_kernel(*refs, m, n_cast, steps_per_tile):
    (x_ref, mod_ref, ng_ref, w_in_ref, lng_ref, lnb_ref, ws_ref, bs_ref, w_out_ref) = refs[:9]
    cast_src = refs[9:9 + n_cast]
    xo_ref = refs[9 + n_cast]
    cast_dst = refs[10 + n_cast:10 + 2 * n_cast]
    h_scr, v_scr, y_scr = refs[10 + 2 * n_cast:]
    rc = min(ROW_CHUNK, m)

    _run_casts(cast_src, cast_dst, steps_per_tile)
    mod = mod_ref[...]
    sh1, sc1, g1 = mod[0:1], mod[1:2], mod[2:3]
    for r0 in range(0, m, rc):
        h_scr[r0:r0 + rc, :] = (
            _rms(x_ref[r0:r0 + rc, :], ng_ref[0:1]) * (1.0 + sc1) + sh1).astype(_BF16)
    h = h_scr[...]

    for hd in range(SGU_HEADS):
        c0 = SGU_WIDTH + hd * SGU_HEAD_DIM
        v_scr[:, hd * SGU_HEAD_DIM:(hd + 1) * SGU_HEAD_DIM] = _gelu(
            _dot(h, w_in_ref[:, c0:c0 + SGU_HEAD_DIM]))
    v = v_scr[...]
    mu = jnp.mean(v, axis=-1, keepdims=True)
    vc = v - mu
    var = jnp.mean(vc * vc, axis=-1, keepdims=True)
    rstd = lax.rsqrt(var + EPS)

    for hd in range(SGU_HEADS):
        cols = slice(hd * SGU_HEAD_DIM, (hd + 1) * SGU_HEAD_DIM)
        vn = ((v_scr[:, cols] - mu) * rstd * lng_ref[:, cols] + lnb_ref[:, cols]).astype(_BF16)
        u = _gelu(_dot(h, w_in_ref[:, cols]))
        w_s = ws_ref[hd]
        bias = bs_ref[:, hd:hd + 1]
        for n in range(m // CHUNK):
            rows = slice(n * CHUNK, (n + 1) * CHUNK)
            s = _dot(w_s, vn[rows]) + bias
            y_scr[rows, cols] = (u[rows] * s).astype(_BF16)

    for r0 in range(0, m, rc):
        rows = slice(r0, r0 + rc)
        o = _dot(y_scr[rows, :], w_out_ref[...])
        xo_ref[rows, :] = x_ref[rows, :] + g1 * _rms(o, ng_ref[1:2])


def _mixer_c(x_all, mod_tbl, norm_g, w_in, lng, lnb, ws, bs_t, w_out, *, layer, m, mod_index,
             casts=None):
    rows, d = x_all.shape
    j = layer // 2
    n_cast = len(casts) if casts is not None else 0
    args = [x_all, mod_tbl, norm_g, w_in, lng, lnb, ws, bs_t, w_out]
    in_specs = [
        pl.BlockSpec((m, d), lambda i: (i, 0)),
        pl.BlockSpec((None, None, N_MOD, d), lambda i: (layer, mod_index(i), 0, 0)),
        _layer_spec(norm_g, layer),
    ] + [_layer_spec(w, None if w.ndim == 2 else j) for w in (w_in, lng, lnb, ws, bs_t, w_out)]
    out_specs = [pl.BlockSpec((m, d), lambda i: (i, 0))]
    out_shape = [jax.ShapeDtypeStruct((rows, d), _F32)]
    if n_cast:
        args += casts.args()
        in_specs += casts.in_specs()
        out_shape += casts.out_shapes()
        out_specs += casts.out_specs()
    res = pl.pallas_call(
        functools.partial(_c_kernel, m=m, n_cast=n_cast,
                          steps_per_tile=casts.spt if n_cast else 1),
        grid=(rows // m,),
        in_specs=in_specs,
        out_specs=out_specs,
        out_shape=out_shape,
        scratch_shapes=[
            pltpu.VMEM((m, d), _BF16),
            pltpu.VMEM((m, SGU_WIDTH), _F32),
            pltpu.VMEM((m, SGU_WIDTH), _BF16),
        ],
        compiler_params=pltpu.CompilerParams(
            dimension_semantics=("arbitrary",), vmem_limit_bytes=VMEM_LIMIT),
        name="mixer_c",
    )(*args)
    return res[0], list(res[1:])


def _ffn_kernel(x_ref, mod_ref, ng_ref, wup_ref, cw_ref, cb_ref, wd_ref, xo_ref, h_scr, act_scr,
                *, m, seg, n_short_blocks, t_long, blk0):
    long_seq = pl.program_id(0) + blk0 >= n_short_blocks
    mod = mod_ref[...]
    sh2, sc2, g2 = mod[3:4], mod[4:5], mod[5:6]
    for r0 in range(0, m, ROW_CHUNK):
        h_scr[r0:r0 + ROW_CHUNK, :] = (
            _rms(x_ref[r0:r0 + ROW_CHUNK, :], ng_ref[2:3]) * (1.0 + sc2) + sh2).astype(_BF16)
    h = h_scr[...]
    tn = COL_GROUP
    sub = lax.broadcasted_iota(jnp.int32, (SUBLANES, tn), 0)

    def conv(z, c0):
        outs = []
        for g in range(m // seg):
            zs = z[g * seg:(g + 1) * seg]
            prev = pltpu.roll(zs, 1, 0)
            nxt = pltpu.roll(zs, seg - 1, 0)
            if g > 0 and t_long > seg:
                before = jnp.where(long_seq, z[g * seg - 1:g * seg], 0.0)
            else:
                before = jnp.zeros((1, tn), _F32)
            if g < m // seg - 1 and t_long > seg:
                after = jnp.where(long_seq, z[(g + 1) * seg:(g + 1) * seg + 1], 0.0)
            else:
                after = jnp.zeros((1, tn), _F32)
            prev = jnp.concatenate(
                [jnp.where(sub == 0, before, prev[:SUBLANES]), prev[SUBLANES:]], axis=0)
            nxt = jnp.concatenate(
                [nxt[:seg - SUBLANES],
                 jnp.where(sub == SUBLANES - 1, after, nxt[seg - SUBLANES:])], axis=0)
            cols = slice(c0, c0 + tn)
            outs.append(cw_ref[0:1, cols] * prev + cw_ref[1:2, cols] * zs
                        + cw_ref[2:3, cols] * nxt + cb_ref[:, cols])
        return jnp.concatenate(outs, axis=0)

    for j in range(D_FF // tn):
        gate = conv(_dot(h, wup_ref[:, j * tn:(j + 1) * tn]), j * tn)
        val = conv(_dot(h, wup_ref[:, D_FF + j * tn:D_FF + (j + 1) * tn]), D_FF + j * tn)
        act_scr[:, j * tn:(j + 1) * tn] = (_gelu(gate) * val).astype(_BF16)

    for r0 in range(0, m, ROW_CHUNK):
        rows = slice(r0, r0 + ROW_CHUNK)
        o = _dot(act_scr[rows, :], wd_ref[...])
        xo_ref[rows, :] = x_ref[rows, :] + g2 * _rms(o, ng_ref[3:4])


def _conv_ffn(x_all, mod_tbl, norm_g, w_up, conv_w, conv_b, w_down, *, layer, m, seg,
              n_short_blocks, t_long, mod_index, blk0=0, nblk=None):
    rows, d = x_all.shape
    if nblk is None:
        nblk = rows // m
    return pl.pallas_call(
        functools.partial(_ffn_kernel, m=m, seg=seg, n_short_blocks=n_short_blocks,
                          t_long=t_long, blk0=blk0),
        grid=(nblk,),
        in_specs=[
            pl.BlockSpec((m, d), lambda i: (i + blk0, 0)),
            pl.BlockSpec((None, None, N_MOD, d), lambda i: (layer, mod_index(i + blk0), 0, 0)),
        ] + [_layer_spec(w, None if w.ndim == 2 else layer)
             for w in (norm_g, w_up, conv_w, conv_b, w_down)],
        out_specs=pl.BlockSpec((m, d), lambda i: (i, 0)),
        out_shape=jax.ShapeDtypeStruct((nblk * m, d), _F32),
        scratch_shapes=[pltpu.VMEM((m, d), _BF16), pltpu.VMEM((m, D_FF), _BF16)],
        compiler_params=pltpu.CompilerParams(
            dimension_semantics=("arbitrary",), vmem_limit_bytes=VMEM_LIMIT),
        name="conv_ffn",
    )(x_all, mod_tbl, norm_g, w_up, conv_w, conv_b, w_down)


def _grid_pos_embedding(t, d):
    rows = t // GRID_W
    row = jnp.repeat(jnp.arange(rows), GRID_W).astype(_F32)
    col = jnp.tile(jnp.arange(GRID_W), rows).astype(_F32)
    quarter = d // 4
    freq = jnp.exp(-math.log(POS_BASE) * jnp.arange(quarter, dtype=_F32) / quarter)

    def enc(p):
        ang = p[:, None] * freq[None, :]
        return jnp.concatenate([jnp.sin(ang), jnp.cos(ang)], axis=-1)

    return jnp.concatenate([enc(row), enc(col)], axis=-1)


def kernel(x_prompt, x_sample, state_lru, c, c_ctx, mod_w, mod_b, norm_g, ab_w_in, lru_conv_w, lru_conv_b, lru_w_gates, lru_b_gates, lru_lambda, conf_conv_w, conf_conv_b, conf_ln_g, conf_ln_b, ab_w_out, c_w_in, c_ln_g, c_ln_b, c_w_s, c_b_s, c_w_out, ffn_w_up, ffn_conv_w, ffn_conv_b, ffn_w_down):
    d = D_MODEL
    batch, seq, _ = x_prompt.shape
    dec_batch, dec_seq, _ = x_sample.shape
    n_ab = ab_w_in.shape[0]
    rows_p = batch * seq
    rows_s = dec_batch * dec_seq

    cond8 = jnp.concatenate(
        [c_ctx[None, :], c, jnp.zeros((MOD_ROWS - 1 - dec_batch, d), _F32)], axis=0)
    mod_tbl = jnp.transpose(_modulation(cond8, mod_w, mod_b), (0, 2, 1, 3))

    def mod_index(block_rows):
        first_latent = rows_p // block_rows
        per_seq = dec_seq // block_rows
        return lambda i: jnp.maximum((i - first_latent) // per_seq + 1, 0)

    wg_b = jnp.transpose(lru_w_gates, (0, 3, 4, 1, 2, 5)).reshape(
        n_ab, LRU_HEADS, LRU_BLOCK, 4 * LRU_BLOCK).astype(_BF16)
    p_ab = (lru_conv_w, lru_conv_b[:, None, :], wg_b, lru_b_gates.reshape(n_ab, 4, LRU_WIDTH),
            lru_lambda, conf_conv_w, conf_conv_b[:, None, :], conf_ln_g[:, None, :],
            conf_ln_b[:, None, :])
    p_c = (c_ln_g[:, None, :], c_ln_b[:, None, :], c_w_s.astype(_BF16),
           jnp.transpose(c_b_s, (0, 2, 1)))

    pos = _grid_pos_embedding(dec_seq, d)
    x_all = jnp.concatenate([x_prompt.reshape(rows_p, d), x_sample.reshape(rows_s, d)], axis=0)
    m_p = 2 * seq
    m_s = dec_seq
    ab_blocks = rows_p // m_p
    c_blocks = (rows_p + rows_s) // C_ROWS
    c_tiles = 8

    mix_in, mix_out = ab_w_in[0].astype(_BF16), ab_w_out[0].astype(_BF16)
    states = []
    for l in range(DEPTH):
        j = l // 2
        items = [(ffn_w_up, l), (ffn_w_down, l)]
        if l % 2 == 0:
            items += [(c_w_in, j), (c_w_out, j)]
            x_all, st, cast = _mixer_ab(
                x_all, None, mod_tbl, norm_g, None, mix_in, *p_ab, mix_out, layer=l, m=m_p, t=seq,
                row0=0, nblk=ab_blocks, mod_row0=0, mod_per_block=False, emit_state=True,
                casts=_Casts(items, ab_blocks, 1))
            states.append(st)
            x_all, _, _ = _mixer_ab(
                x_all, pos if l == 0 else None, mod_tbl, norm_g, state_lru, mix_in, *p_ab, mix_out,
                layer=l, m=m_s, t=dec_seq, row0=rows_p, nblk=dec_batch, mod_row0=1,
                mod_per_block=True, emit_state=False)
        else:
            if l + 1 < DEPTH:
                items += [(ab_w_in, j + 1), (ab_w_out, j + 1)]
            x_all, cast = _mixer_c(
                x_all, mod_tbl, norm_g, mix_in, *p_c, mix_out, layer=l, m=C_ROWS,
                mod_index=mod_index(C_ROWS), casts=_Casts(items, c_tiles, c_blocks // c_tiles))
        w_up_b, w_down_b = cast[0], cast[1]
        if len(cast) > 2:
            mix_in, mix_out = cast[2], cast[3]
        ffn = functools.partial(
            _conv_ffn, x_all, mod_tbl, norm_g, w_up_b, ffn_conv_w, ffn_conv_b[:, None, :],
            w_down_b, layer=l, m=FFN_ROWS, seg=seq, n_short_blocks=rows_p // FFN_ROWS,
            t_long=dec_seq, mod_index=mod_index(FFN_ROWS))
        if l + 1 < DEPTH:
            x_all = ffn()
        else:
            y_p = ffn(blk0=0, nblk=rows_p // FFN_ROWS)
            y_s = ffn(blk0=rows_p // FFN_ROWS, nblk=rows_s // FFN_ROWS)

    y_prompt = y_p.reshape(batch, seq, d)
    y_sample = y_s.reshape(dec_batch, dec_seq, d)
    new_state = jnp.stack(states, axis=1)
    return (y_prompt, y_sample, new_state)
```

```python
import functools
import math

import jax
import jax.numpy as jnp
import numpy as np
from jax import lax
from jax.experimental import pallas as pl
from jax.experimental.pallas import tpu as pltpu

D_MODEL = 1024
DEPTH = 4
GRID_W = 64
LRU_WIDTH = 1024
LRU_HEADS = 8
LRU_BLOCK = LRU_WIDTH // LRU_HEADS
LRU_CONV = 4
LRU_C = 8.0
CONF_WIDTH = 1024
CONF_KERNEL = 31
CHUNK = 128
SGU_WIDTH = 2048
SGU_HEADS = 8
SGU_HEAD_DIM = SGU_WIDTH // SGU_HEADS
D_FF = 2816
FFN_CONV = 3
N_MOD = 6
EPS = 1e-6
POS_BASE = 10000.0

SUBLANES = 8
LANES = 128
BF16_ROWS = 16
COL_GROUP = 256
N_GROUPS = LRU_WIDTH // COL_GROUP
MOD_ROWS = 8
LRU_LEFT = LRU_CONV // 2
CONF_HALF = CONF_KERNEL // 2
CONF_ROWS = 64
ROW_CHUNK = 256
SCAN_UNROLL = 8
LOOP_UNROLL = 4
FFN_ROWS = 1024
C_ROWS = 512
VMEM_LIMIT = 56 * 1024 * 1024

_BF16 = jnp.bfloat16
_F32 = jnp.float32
_LOG2E = math.log2(math.e)
_GELU_K1 = -2.0 * math.sqrt(2.0 / math.pi) * _LOG2E
_GELU_K2 = _GELU_K1 * 0.044715


def _rms(x, g):
    return x * lax.rsqrt(jnp.mean(x * x, axis=-1, keepdims=True) + EPS) * g


def _dot(a, b):
    return jnp.dot(a, b, preferred_element_type=_F32)


def _sigmoid(x):
    return 1.0 / (1.0 + jnp.exp2(x * (-_LOG2E)))


def _gelu(x):
    return x / (1.0 + jnp.exp2(x * (_GELU_K1 + _GELU_K2 * (x * x))))


def _layer_spec(arr, layer):
    if layer is None:
        zeros = (0,) * arr.ndim
        return pl.BlockSpec(arr.shape, lambda *_: zeros, pipeline_mode=pl.Buffered(1))
    tail = (0,) * (arr.ndim - 1)
    return pl.BlockSpec((None,) + arr.shape[1:], lambda *_: (layer,) + tail,
                        pipeline_mode=pl.Buffered(1))


class _Casts:
    def __init__(self, items, ntiles, steps_per_tile):
        self.items = items
        self.ntiles = ntiles
        self.spt = steps_per_tile

    def __len__(self):
        return len(self.items)

    def args(self):
        return [arr for arr, _ in self.items]

    def in_specs(self):
        spt = self.spt
        return [pl.BlockSpec((None, arr.shape[1] // self.ntiles, arr.shape[2]),
                             functools.partial(lambda i, layer: (layer, i // spt, 0), layer=layer))
                for arr, layer in self.items]

    def out_specs(self):
        spt = self.spt
        return [pl.BlockSpec((arr.shape[1] // self.ntiles, arr.shape[2]), lambda i: (i // spt, 0))
                for arr, _ in self.items]

    def out_shapes(self):
        return [jax.ShapeDtypeStruct(arr.shape[1:], _BF16) for arr, _ in self.items]


def _run_casts(src_refs, dst_refs, steps_per_tile):
    def cast():
        for src, dst in zip(src_refs, dst_refs):
            dst[...] = src[...].astype(_BF16)

    if not src_refs:
        return
    if steps_per_tile == 1:
        cast()
    else:
        pl.when(pl.program_id(0) % steps_per_tile == 0)(cast)


def _mod_kernel(cond_ref, w_ref, b_ref, o_ref):
    c = cond_ref[...]
    s = (c * _sigmoid(c)).astype(_BF16)
    o_ref[0, 0] = _dot(s, w_ref[0].astype(_BF16)) + b_ref[0, 0]


def _modulation(cond8, mod_w, mod_b):
    d = D_MODEL
    return pl.pallas_call(
        _mod_kernel,
        grid=(DEPTH, N_MOD),
        in_specs=[
            pl.BlockSpec((MOD_ROWS, d), lambda l, n: (0, 0)),
            pl.BlockSpec((1, d, d), lambda l, n: (l, 0, n)),
            pl.BlockSpec((1, 1, 1, d), lambda l, n: (l, n, 0, 0)),
        ],
        out_specs=pl.BlockSpec((1, 1, MOD_ROWS, d), lambda l, n: (l, n, 0, 0)),
        out_shape=jax.ShapeDtypeStruct((DEPTH, N_MOD, MOD_ROWS, d), _F32),
        compiler_params=pltpu.CompilerParams(
            dimension_semantics=("arbitrary", "arbitrary"), vmem_limit_bytes=VMEM_LIMIT),
        name="modulation",
    )(cond8, mod_w, mod_b.reshape(DEPTH, N_MOD, 1, d))


def _ab_kernel(*refs, m, t, has_pos, has_h0, emit_state, n_cast):
    nseq = m // t
    lc = t // SUBLANES
    nplane = D_MODEL // LANES
    rc = min(ROW_CHUNK, t)
    conv_rows = CONF_ROWS // nseq
    refs = list(refs)
    x_ref = refs.pop(0)
    pos_ref = refs.pop(0) if has_pos else None
    perm_ref = refs.pop(0)
    mod_ref = refs.pop(0)
    ng_ref = refs.pop(0)
    h0_ref = refs.pop(0) if has_h0 else None
    (w_in_ref, cw_ref, cb_ref, wg_ref, bg_ref, lam_ref,
     ccw_ref, ccb_ref, lng_ref, lnb_ref, w_out_ref) = refs[:11]
    cast_src = refs[11:11 + n_cast]
    refs = refs[11 + n_cast:]
    xo_ref = refs.pop(0)
    st_ref = refs.pop(0) if emit_state else None
    cast_dst = refs[:n_cast]
    (h_scr, ycat_scr, plane_scr, af_scr, bf_scr, ab_scr, bb_scr, zg_scr, za_scr, zb_scr,
     xext_scr, gext_scr, wtap_scr) = refs[n_cast:]

    _run_casts(cast_src, cast_dst, 1)

    def residual(rows, cols=slice(None)):
        if has_pos:
            return x_ref[rows, cols] + pos_ref[rows, cols]
        return x_ref[rows, cols]

    mod = mod_ref[...]
    sh1, sc1, g1 = mod[0:1], mod[1:2], mod[2:3]

    for r0 in range(0, m, rc):
        hn = _rms(residual(slice(r0, r0 + rc)), ng_ref[0:1]) * (1.0 + sc1) + sh1
        ycat_scr[r0:r0 + rc, :D_MODEL] = hn.astype(_BF16)
    for r0 in range(0, m, rc):
        q, rq = divmod(r0, t)
        h_scr[r0:r0 + rc, :] = _dot(perm_ref[rq:rq + rc, :],
                                    ycat_scr[q * t:(q + 1) * t, :D_MODEL]).astype(_BF16)

    neg_lam = -lam_ref[...]
    softplus = jnp.maximum(neg_lam, 0.0) + jnp.log1p(jnp.exp(-jnp.abs(neg_lam)))
    decay = softplus * (-LRU_C * _LOG2E)

    sub = lax.broadcasted_iota(jnp.int32, (SUBLANES, COL_GROUP), 0)

    def front_halo(v):
        s_idx = lax.broadcasted_iota(jnp.int32, v.shape, 0) & (SUBLANES - 1)
        return jnp.where(s_idx == 0, 0.0, pltpu.roll(v, 1, 0))

    def back_halo(v):
        s_idx = lax.broadcasted_iota(jnp.int32, v.shape, 0) & (SUBLANES - 1)
        return jnp.where(s_idx == SUBLANES - 1, 0.0, pltpu.roll(v, v.shape[0] - 1, 0))

    def fill_halos(ext_scr, halo):
        hr = halo * SUBLANES
        for q in range(nseq):
            ext_scr[q, 0:hr, :] = front_halo(ext_scr[q, t:t + hr, :])
            ext_scr[q, hr + t:, :] = back_halo(ext_scr[q, hr:2 * hr, :])

    def project(dst_ref, row_slice, r0, col0):
        dst_ref[row_slice] = _dot(h_scr[r0:r0 + rc, :], w_in_ref[:, col0:col0 + COL_GROUP])

    def project_x(grp, r0):
        q, rq = divmod(r0, t)
        hr = LRU_LEFT * SUBLANES
        project(xext_scr, (q, slice(hr + rq, hr + rq + rc), slice(None)), r0,
                LRU_WIDTH + grp * COL_GROUP)

    for r0 in range(0, m, rc):
        project_x(0, r0)
    fill_halos(xext_scr, LRU_LEFT)

    for grp in range(N_GROUPS):
        c0 = grp * COL_GROUP
        cols = slice(c0, c0 + COL_GROUP)

        heads_per_group = COL_GROUP // LRU_BLOCK
        for r0 in range(0, m, rc):
            q, rq = divmod(r0, t)
            project(zg_scr, slice(r0, r0 + rc), r0, c0)
            project(za_scr, slice(r0, r0 + rc), r0, 2 * LRU_WIDTH + c0)
            project(zb_scr, slice(r0, r0 + rc), r0, 2 * LRU_WIDTH + CONF_WIDTH + c0)
            xc = jnp.broadcast_to(cb_ref[:, cols], (rc, COL_GROUP))
            for k in range(LRU_CONV):
                off = rq + k * SUBLANES
                xc = xc + cw_ref[k:k + 1, cols] * xext_scr[q, off:off + rc, :]
            xcb = xc.astype(_BF16)
            gl_parts = [
                _dot(xcb[:, hh * LRU_BLOCK:(hh + 1) * LRU_BLOCK],
                     wg_ref[grp * heads_per_group + hh])
                for hh in range(heads_per_group)]

            def gate(dg):
                parts = [gp[:, dg * LRU_BLOCK:(dg + 1) * LRU_BLOCK] for gp in gl_parts]
                return _sigmoid(jnp.concatenate(parts, axis=1) + bg_ref[dg:dg + 1, cols])

            for d, (a_scr, b_scr) in enumerate(((af_scr, bf_scr), (ab_scr, bb_scr))):
                r = gate(2 * d)
                i_gate = gate(2 * d + 1)
                a = jnp.exp2(r * decay[d:d + 1, cols])
                y = 1.0 - a * a
                a_scr[r0:r0 + rc, :] = a
                b_scr[r0:r0 + rc, :] = jnp.where(y > 0.0, y * lax.rsqrt(y), 0.0) * i_gate * xc

        def scan_body(i, carry):
            out = []
            for q in range(nseq):
                hf, df, hb, db = carry[q]
                rows_f = pl.ds(pl.multiple_of(q * t + i * SUBLANES, SUBLANES), SUBLANES)
                rows_b = pl.ds(pl.multiple_of(q * t + (lc - 1 - i) * SUBLANES, SUBLANES), SUBLANES)
                a = af_scr[rows_f, :]
                hf = a * hf + bf_scr[rows_f, :]
                df = df * a
                bf_scr[rows_f, :] = hf
                af_scr[rows_f, :] = df
                a = ab_scr[rows_b, :]
                hb = a * hb + bb_scr[rows_b, :]
                db = db * a
                bb_scr[rows_b, :] = hb
                ab_scr[rows_b, :] = db
                out.append((hf, df, hb, db))
            return tuple(out)

        zeros8 = jnp.zeros((SUBLANES, COL_GROUP), _F32)
        ones8 = jnp.ones((SUBLANES, COL_GROUP), _F32)
        ends = lax.fori_loop(0, lc, scan_body,
                             tuple((zeros8, ones8, zeros8, ones8) for _ in range(nseq)),
                             unroll=SCAN_UNROLL)

        entry = []
        for q in range(nseq):
            hf_end, df_end, hb_end, db_end = ends[q]
            if has_h0:
                cf = h0_ref[q, 0:1, cols]
                cb = h0_ref[q, 1:2, cols]
            else:
                cf = jnp.zeros((1, COL_GROUP), _F32)
                cb = cf
            cf_all = jnp.zeros((SUBLANES, COL_GROUP), _F32)
            for s in range(SUBLANES):
                cf_all = jnp.where(sub == s, cf, cf_all)
                cf = hf_end[s:s + 1] + df_end[s:s + 1] * cf
            cb_all = jnp.zeros((SUBLANES, COL_GROUP), _F32)
            for s in reversed(range(SUBLANES)):
                cb_all = jnp.where(sub == s, cb, cb_all)
                cb = hb_end[s:s + 1] + db_end[s:s + 1] * cb
            if emit_state:
                st_ref[q, 0:1, cols] = cf
                st_ref[q, 1:2, cols] = cb
            entry.append((cf_all, cb_all))
        gr = CONF_HALF * SUBLANES
        for r0 in range(0, m, rc):
            q, rq = divmod(r0, t)
            rows = slice(r0, r0 + rc)
            if grp + 1 < N_GROUPS:
                project_x(grp + 1, r0)
            cf_all, cb_all = entry[q]
            shape3 = (rc // SUBLANES, SUBLANES, COL_GROUP)
            hsum = (bf_scr[rows, :].reshape(shape3) + af_scr[rows, :].reshape(shape3) * cf_all[None]
                    + bb_scr[rows, :].reshape(shape3) + ab_scr[rows, :].reshape(shape3) * cb_all[None])
            ycat_scr[rows, cols] = (
                _gelu(zg_scr[rows, :]) * hsum.reshape(rc, COL_GROUP)).astype(_BF16)
            gext_scr[q, gr + rq:gr + rq + rc, :] = za_scr[rows, :] * _sigmoid(zb_scr[rows, :])
        fill_halos(gext_scr, CONF_HALF)
        if grp + 1 < N_GROUPS:
            fill_halos(xext_scr, LRU_LEFT)

        for k in range(CONF_KERNEL):
            wtap_scr[k] = jnp.broadcast_to(ccw_ref[k:k + 1, cols], (SUBLANES, COL_GROUP))
        shape3 = (conv_rows // SUBLANES, SUBLANES, COL_GROUP)

        def conv_body(i, _):
            r0 = pl.multiple_of(i * conv_rows, conv_rows)
            for q in range(nseq):
                acc = jnp.broadcast_to(ccb_ref[:, cols], shape3)
                for k in range(CONF_KERNEL):
                    tile = gext_scr[q, pl.ds(r0 + k * SUBLANES, conv_rows), :]
                    acc = acc + wtap_scr[k][None] * tile.reshape(shape3)
                acc = acc.reshape(conv_rows, COL_GROUP)
                for v in range(COL_GROUP // LANES):
                    plane_scr[grp * (COL_GROUP // LANES) + v, pl.ds(q * t + r0, conv_rows), :] = (
                        acc[:, v * LANES:(v + 1) * LANES])
            return 0

        lax.fori_loop(0, t // conv_rows, conv_body, 0, unroll=2)

    for r0 in range(0, m, rc):
        rows = slice(r0, r0 + rc)
        gc = jnp.concatenate([plane_scr[p, rows, :] for p in range(nplane)], axis=1)
        mu = jnp.mean(gc, axis=-1, keepdims=True)
        gcc = gc - mu
        var = jnp.mean(gcc * gcc, axis=-1, keepdims=True)
        yb = gcc * lax.rsqrt(var + EPS) * lng_ref[...] + lnb_ref[...]
        ycat_scr[rows, LRU_WIDTH:] = (yb * _sigmoid(yb)).astype(_BF16)
        upd = g1 * _rms(_dot(ycat_scr[rows, :], w_out_ref[...]), ng_ref[1:2])
        for p in range(nplane):
            plane_scr[p, rows, :] = upd[:, p * LANES:(p + 1) * LANES]

    def unpermute_body(i, _):
        for q in range(nseq):
            for s in range(SUBLANES):
                dst = pl.ds(pl.multiple_of(q * t + s * lc + i * SUBLANES, SUBLANES), SUBLANES)
                for p in range(nplane):
                    u = plane_scr[p, pl.ds(q * t + i * SUBLANES * SUBLANES + s, SUBLANES,
                                          stride=SUBLANES), :]
                    pc = slice(p * LANES, (p + 1) * LANES)
                    xo_ref[dst, pc] = residual(dst, pc) + u
        return 0

    lax.fori_loop(0, lc // SUBLANES, unpermute_body, 0, unroll=LOOP_UNROLL)


def _mixer_ab(x_all, pos, mod_tbl, norm_g, h0_all, w_in, cw, cb, wg, bg, lam, ccw, ccb, lng, lnb,
              w_out, *, layer, m, t, row0, nblk, mod_row0, mod_per_block, emit_state, casts=None):
    rows, d = x_all.shape
    nseq = m // t
    blk0 = row0 // m
    j = layer // 2
    has_pos = pos is not None
    has_h0 = h0_all is not None
    n_cast = len(casts) if casts is not None else 0
    single = dict(pipeline_mode=pl.Buffered(1))
    big = m * d * 4 > 2 * 1024 * 1024
    args = [x_all]
    in_specs = [pl.BlockSpec((m, d), lambda i: (i + blk0, 0), **(single if big and has_pos else {}))]
    if has_pos:
        args.append(pos)
        in_specs.append(pl.BlockSpec(pos.shape, lambda i: (0, 0), **single))
    idx = np.arange(t)
    perm = np.zeros((t, t), np.float32)
    perm[idx, (idx % SUBLANES) * (t // SUBLANES) + idx // SUBLANES] = 1.0
    args.append(jnp.asarray(perm, _BF16))
    in_specs.append(pl.BlockSpec((t, t), lambda i: (0, 0), **single))
    args.append(mod_tbl)
    in_specs.append(pl.BlockSpec(
        (None, None, N_MOD, d),
        (lambda i: (layer, i + mod_row0, 0, 0)) if mod_per_block
        else (lambda i: (layer, mod_row0, 0, 0))))
    args.append(norm_g)
    in_specs.append(_layer_spec(norm_g, layer))
    if has_h0:
        args.append(h0_all)
        in_specs.append(pl.BlockSpec((nseq, None, 2, LRU_WIDTH), lambda i: (i, j, 0, 0)))
    for w in (w_in, cw, cb, wg, bg, lam, ccw, ccb, lng, lnb, w_out):
        args.append(w)
        in_specs.append(_layer_spec(w, None if w.ndim == 2 else j))
    out_shape = [jax.ShapeDtypeStruct((rows, d), _F32)]
    out_specs = [pl.BlockSpec((m, d), lambda i: (i + blk0, 0), **(single if big else {}))]
    if emit_state:
        out_shape.append(jax.ShapeDtypeStruct((nblk * nseq, 2, LRU_WIDTH), _F32))
        out_specs.append(pl.BlockSpec((nseq, 2, LRU_WIDTH), lambda i: (i, 0, 0)))
    if n_cast:
        args += casts.args()
        in_specs += casts.in_specs()
        out_shape += casts.out_shapes()
        out_specs += casts.out_specs()
    scratch = [
        pltpu.VMEM((m, d), _BF16),
        pltpu.VMEM((m, LRU_WIDTH + CONF_WIDTH), _BF16),
        pltpu.VMEM((d // LANES, m, LANES), _F32),
    ] + [pltpu.VMEM((m, COL_GROUP), _F32)] * 7 + [
        pltpu.VMEM((nseq, t + 2 * LRU_LEFT * SUBLANES, COL_GROUP), _F32),
        pltpu.VMEM((nseq, t + 2 * CONF_HALF * SUBLANES, COL_GROUP), _F32),
        pltpu.VMEM((CONF_KERNEL, SUBLANES, COL_GROUP), _F32),
    ]
    res = pl.pallas_call(
        functools.partial(_ab_kernel, m=m, t=t, has_pos=has_pos, has_h0=has_h0,
                          emit_state=emit_state, n_cast=n_cast),
        grid=(nblk,),
        in_specs=in_specs,
        out_specs=out_specs,
        out_shape=out_shape,
        scratch_shapes=scratch,
        input_output_aliases={0: 0},
        compiler_params=pltpu.CompilerParams(
            dimension_semantics=("arbitrary",), vmem_limit_bytes=VMEM_LIMIT),
        name="mixer_ab",
    )(*args)
    n_main = 2 if emit_state else 1
    return res[0], (res[1] if emit_state else None), list(res[n_main:])


def _c_kernel(*refs, m, n_cast, steps_per_tile):
    (x_ref, mod_ref, ng_ref, w_in_ref, lng_ref, lnb_ref, ws_ref, bs_ref, w_out_ref) = refs[:9]
    cast_src = refs[9:9 + n_cast]
    xo_ref = refs[9 + n_cast]
    cast_dst = refs[10 + n_cast:10 + 2 * n_cast]
    h_scr, v_scr, y_scr = refs[10 + 2 * n_cast:]
    rc = min(ROW_CHUNK, m)

    _run_casts(cast_src, cast_dst, steps_per_tile)
    mod = mod_ref[...]
    sh1, sc1, g1 = mod[0:1], mod[1:2], mod[2:3]
    for r0 in range(0, m, rc):
        h_scr[r0:r0 + rc, :] = (
            _rms(x_ref[r0:r0 + rc, :], ng_ref[0:1]) * (1.0 + sc1) + sh1).astype(_BF16)
    h = h_scr[...]

    for hd in range(SGU_HEADS):
        c0 = SGU_WIDTH + hd * SGU_HEAD_DIM
        v_scr[:, hd * SGU_HEAD_DIM:(hd + 1) * SGU_HEAD_DIM] = _gelu(
            _dot(h, w_in_ref[:, c0:c0 + SGU_HEAD_DIM]))
    v = v_scr[...]
    mu = jnp.mean(v, axis=-1, keepdims=True)
    vc = v - mu
    var = jnp.mean(vc * vc, axis=-1, keepdims=True)
    rstd = lax.rsqrt(var + EPS)

    for hd in range(SGU_HEADS):
        cols = slice(hd * SGU_HEAD_DIM, (hd + 1) * SGU_HEAD_DIM)
        vn = ((v_scr[:, cols] - mu) * rstd * lng_ref[:, cols] + lnb_ref[:, cols]).astype(_BF16)
        u = _gelu(_dot(h, w_in_ref[:, cols]))
        w_s = ws_ref[hd]
        bias = bs_ref[:, hd:hd + 1]
        for n in range(m // CHUNK):
            rows = slice(n * CHUNK, (n + 1) * CHUNK)
            s = _dot(w_s, vn[rows]) + bias
            y_scr[rows, cols] = (u[rows] * s).astype(_BF16)

    for r0 in range(0, m, rc):
        rows = slice(r0, r0 + rc)
        o = _dot(y_scr[rows, :], w_out_ref[...])
        xo_ref[rows, :] = x_ref[rows, :] + g1 * _rms(o, ng_ref[1:2])


def _mixer_c(x_all, mod_tbl, norm_g, w_in, lng, lnb, ws, bs_t, w_out, *, layer, m, mod_index,
             casts=None):
    rows, d = x_all.shape
    j = layer // 2
    n_cast = len(casts) if casts is not None else 0
    args = [x_all, mod_tbl, norm_g, w_in, lng, lnb, ws, bs_t, w_out]
    in_specs = [
        pl.BlockSpec((m, d), lambda i: (i, 0)),
        pl.BlockSpec((None, None, N_MOD, d), lambda i: (layer, mod_index(i), 0, 0)),
        _layer_spec(norm_g, layer),
    ] + [_layer_spec(w, None if w.ndim == 2 else j) for w in (w_in, lng, lnb, ws, bs_t, w_out)]
    out_specs = [pl.BlockSpec((m, d), lambda i: (i, 0))]
    out_shape = [jax.ShapeDtypeStruct((rows, d), _F32)]
    if n_cast:
        args += casts.args()
        in_specs += casts.in_specs()
        out_shape += casts.out_shapes()
        out_specs += casts.out_specs()
    res = pl.pallas_call(
        functools.partial(_c_kernel, m=m, n_cast=n_cast,
                          steps_per_tile=casts.spt if n_cast else 1),
        grid=(rows // m,),
        in_specs=in_specs,
        out_specs=out_specs,
        out_shape=out_shape,
        scratch_shapes=[
            pltpu.VMEM((m, d), _BF16),
            pltpu.VMEM((m, SGU_WIDTH), _F32),
            pltpu.VMEM((m, SGU_WIDTH), _BF16),
        ],
        compiler_params=pltpu.CompilerParams(
            dimension_semantics=("arbitrary",), vmem_limit_bytes=VMEM_LIMIT),
        name="mixer_c",
    )(*args)
    return res[0], list(res[1:])


def _ffn_kernel(x_ref, mod_ref, ng_ref, wup_ref, cw_ref, cb_ref, wd_ref, xo_ref, h_scr, act_scr,
                *, m, seg, n_short_blocks, t_long, blk0):
    long_seq = pl.program_id(0) + blk0 >= n_short_blocks
    mod = mod_ref[...]
    sh2, sc2, g2 = mod[3:4], mod[4:5], mod[5:6]
    for r0 in range(0, m, ROW_CHUNK):
        h_scr[r0:r0 + ROW_CHUNK, :] = (
            _rms(x_ref[r0:r0 + ROW_CHUNK, :], ng_ref[2:3]) * (1.0 + sc2) + sh2).astype(_BF16)
    h = h_scr[...]
    tn = COL_GROUP
    sub = lax.broadcasted_iota(jnp.int32, (SUBLANES, tn), 0)

    def conv(z, c0):
        outs = []
        for g in range(m // seg):
            zs = z[g * seg:(g + 1) * seg]
            prev = pltpu.roll(zs, 1, 0)
            nxt = pltpu.roll(zs, seg - 1, 0)
            if g > 0 and t_long > seg:
                before = jnp.where(long_seq, z[g * seg - 1:g * seg], 0.0)
            else:
                before = jnp.zeros((1, tn), _F32)
            if g < m // seg - 1 and t_long > seg:
                after = jnp.where(long_seq, z[(g + 1) * seg:(g + 1) * seg + 1], 0.0)
            else:
                after = jnp.zeros((1, tn), _F32)
            prev = jnp.concatenate(
                [jnp.where(sub == 0, before, prev[:SUBLANES]), prev[SUBLANES:]], axis=0)
            nxt = jnp.concatenate(
                [nxt[:seg - SUBLANES],
                 jnp.where(sub == SUBLANES - 1, after, nxt[seg - SUBLANES:])], axis=0)
            cols = slice(c0, c0 + tn)
            outs.append(cw_ref[0:1, cols] * prev + cw_ref[1:2, cols] * zs
                        + cw_ref[2:3, cols] * nxt + cb_ref[:, cols])
        return jnp.concatenate(outs, axis=0)

    for j in range(D_FF // tn):
        gate = conv(_dot(h, wup_ref[:, j * tn:(j + 1) * tn]), j * tn)
        val = conv(_dot(h, wup_ref[:, D_FF + j * tn:D_FF + (j + 1) * tn]), D_FF + j * tn)
        act_scr[:, j * tn:(j + 1) * tn] = (_gelu(gate) * val).astype(_BF16)

    for r0 in range(0, m, ROW_CHUNK):
        rows = slice(r0, r0 + ROW_CHUNK)
        o = _dot(act_scr[rows, :], wd_ref[...])
        xo_ref[rows, :] = x_ref[rows, :] + g2 * _rms(o, ng_ref[3:4])


def _conv_ffn(x_all, mod_tbl, norm_g, w_up, conv_w, conv_b, w_down, *, layer, m, seg,
              n_short_blocks, t_long, mod_index, blk0=0, nblk=None):
    rows, d = x_all.shape
    if nblk is None:
        nblk = rows // m
    return pl.pallas_call(
        functools.partial(_ffn_kernel, m=m, seg=seg, n_short_blocks=n_short_blocks,
                          t_long=t_long, blk0=blk0),
        grid=(nblk,),
        in_specs=[
            pl.BlockSpec((m, d), lambda i: (i + blk0, 0)),
            pl.BlockSpec((None, None, N_MOD, d), lambda i: (layer, mod_index(i + blk0), 0, 0)),
        ] + [_layer_spec(w, None if w.ndim == 2 else layer)
             for w in (norm_g, w_up, conv_w, conv_b, w_down)],
        out_specs=pl.BlockSpec((m, d), lambda i: (i, 0)),
        out_shape=jax.ShapeDtypeStruct((nblk * m, d), _F32),
        scratch_shapes=[pltpu.VMEM((m, d), _BF16), pltpu.VMEM((m, D_FF), _BF16)],
        compiler_params=pltpu.CompilerParams(
            dimension_semantics=("arbitrary",), vmem_limit_bytes=VMEM_LIMIT),
        name="conv_ffn",
    )(x_all, mod_tbl, norm_g, w_up, conv_w, conv_b, w_down)


def _grid_pos_embedding(t, d):
    rows = t // GRID_W
    row = jnp.repeat(jnp.arange(rows), GRID_W).astype(_F32)
    col = jnp.tile(jnp.arange(GRID_W), rows).astype(_F32)
    quarter = d // 4
    freq = jnp.exp(-math.log(POS_BASE) * jnp.arange(quarter, dtype=_F32) / quarter)

    def enc(p):
        ang = p[:, None] * freq[None, :]
        return jnp.concatenate([jnp.sin(ang), jnp.cos(ang)], axis=-1)

    return jnp.concatenate([enc(row), enc(col)], axis=-1)


def kernel(x_prompt, x_sample, state_lru, c, c_ctx, mod_w, mod_b, norm_g, ab_w_in, lru_conv_w, lru_conv_b, lru_w_gates, lru_b_gates, lru_lambda, conf_conv_w, conf_conv_b, conf_ln_g, conf_ln_b, ab_w_out, c_w_in, c_ln_g, c_ln_b, c_w_s, c_b_s, c_w_out, ffn_w_up, ffn_conv_w, ffn_conv_b, ffn_w_down):
    d = D_MODEL
    batch, seq, _ = x_prompt.shape
    dec_batch, dec_seq, _ = x_sample.shape
    n_ab = ab_w_in.shape[0]
    rows_p = batch * seq
    rows_s = dec_batch * dec_seq

    cond8 = jnp.concatenate(
        [c_ctx[None, :], c, jnp.zeros((MOD_ROWS - 1 - dec_batch, d), _F32)], axis=0)
    mod_tbl = jnp.transpose(_modulation(cond8, mod_w, mod_b), (0, 2, 1, 3))

    def mod_index(block_rows):
        first_latent = rows_p // block_rows
        per_seq = dec_seq // block_rows
        return lambda i: jnp.maximum((i - first_latent) // per_seq + 1, 0)

    wg_b = jnp.transpose(lru_w_gates, (0, 3, 4, 1, 2, 5)).reshape(
        n_ab, LRU_HEADS, LRU_BLOCK, 4 * LRU_BLOCK).astype(_BF16)
    p_ab = (lru_conv_w, lru_conv_b[:, None, :], wg_b, lru_b_gates.reshape(n_ab, 4, LRU_WIDTH),
            lru_lambda, conf_conv_w, conf_conv_b[:, None, :], conf_ln_g[:, None, :],
            conf_ln_b[:, None, :])
    p_c = (c_ln_g[:, None, :], c_ln_b[:, None, :], c_w_s.astype(_BF16),
           jnp.transpose(c_b_s, (0, 2, 1)))

    pos = _grid_pos_embedding(dec_seq, d)
    x_all = jnp.concatenate([x_prompt.reshape(rows_p, d), x_sample.reshape(rows_s, d)], axis=0)
    m_p = 2 * seq
    m_s = dec_seq
    ab_blocks = rows_p // m_p
    c_blocks = (rows_p + rows_s) // C_ROWS
    c_tiles = 8

    mix_in, mix_out = ab_w_in[0].astype(_BF16), ab_w_out[0].astype(_BF16)
    states = []
    for l in range(DEPTH):
        j = l // 2
        items = [(ffn_w_up, l), (ffn_w_down, l)]
        if l % 2 == 0:
            items += [(c_w_in, j), (c_w_out, j)]
            x_all, st, cast = _mixer_ab(
                x_all, None, mod_tbl, norm_g, None, mix_in, *p_ab, mix_out, layer=l, m=m_p, t=seq,
                row0=0, nblk=ab_blocks, mod_row0=0, mod_per_block=False, emit_state=True,
                casts=_Casts(items, ab_blocks, 1))
            states.append(st)
            x_all, _, _ = _mixer_ab(
                x_all, pos if l == 0 else None, mod_tbl, norm_g, state_lru, mix_in, *p_ab, mix_out,
                layer=l, m=m_s, t=dec_seq, row0=rows_p, nblk=dec_batch, mod_row0=1,
                mod_per_block=True, emit_state=False)
        else:
            if l + 1 < DEPTH:
                items += [(ab_w_in, j + 1), (ab_w_out, j + 1)]
            x_all, cast = _mixer_c(
                x_all, mod_tbl, norm_g, mix_in, *p_c, mix_out, layer=l, m=C_ROWS,
                mod_index=mod_index(C_ROWS), casts=_Casts(items, c_tiles, c_blocks // c_tiles))
        w_up_b, w_down_b = cast[0], cast[1]
        if len(cast) > 2:
            mix_in, mix_out = cast[2], cast[3]
        ffn = functools.partial(
            _conv_ffn, x_all, mod_tbl, norm_g, w_up_b, ffn_conv_w, ffn_conv_b[:, None, :],
            w_down_b, layer=l, m=FFN_ROWS, seg=seq, n_short_blocks=rows_p // FFN_ROWS,
            t_long=dec_seq, mod_index=mod_index(FFN_ROWS))
        if l + 1 < DEPTH:
            x_all = ffn()
        else:
            y_p = ffn(blk0=0, nblk=rows_p // FFN_ROWS)
            y_s = ffn(blk0=rows_p // FFN_ROWS, nblk=rows_s // FFN_ROWS)

    y_prompt = y_p.reshape(batch, seq, d)
    y_sample = y_s.reshape(dec_batch, dec_seq, d)
    new_state = jnp.stack(states, axis=1)
    return (y_prompt, y_sample, new_state)
```

```python
import functools
import math

import jax
import jax.numpy as jnp
import numpy as np
from jax import lax
from jax.experimental import pallas as pl
from jax.experimental.pallas import tpu as pltpu

D_MODEL = 1024
DEPTH = 4
GRID_W = 64
LRU_WIDTH = 1024
LRU_HEADS = 8
LRU_BLOCK = LRU_WIDTH // LRU_HEADS
LRU_CONV = 4
LRU_C = 8.0
CONF_WIDTH = 1024
CONF_KERNEL = 31
CHUNK = 128
SGU_WIDTH = 2048
SGU_HEADS = 8
SGU_HEAD_DIM = SGU_WIDTH // SGU_HEADS
D_FF = 2816
FFN_CONV = 3
N_MOD = 6
EPS = 1e-6
POS_BASE = 10000.0

SUBLANES = 8
LANES = 128
COL_GROUP = 256
N_GROUPS = LRU_WIDTH // COL_GROUP
MOD_ROWS = 8
LRU_LEFT = LRU_CONV // 2
CONF_HALF = CONF_KERNEL // 2
CONF_ROWS = 64
ROW_CHUNK = 256
SCAN_UNROLL = 8
LOOP_UNROLL = 4
C_EARLY_HEADS = 3
FFN_ROWS = 1024
C_ROWS = 512
VMEM_LIMIT = 56 * 1024 * 1024

_BF16 = jnp.bfloat16
_F32 = jnp.float32
_LOG2E = math.log2(math.e)
_GELU_K1 = -2.0 * math.sqrt(2.0 / math.pi) * _LOG2E
_GELU_K2 = _GELU_K1 * 0.044715


def _rms(x, g):
    return x * lax.rsqrt(jnp.mean(x * x, axis=-1, keepdims=True) + EPS) * g


def _dot(a, b):
    return jnp.dot(a, b, preferred_element_type=_F32)


def _sigmoid(x):
    return 1.0 / (1.0 + jnp.exp2(x * (-_LOG2E)))


def _gelu(x):
    return x / (1.0 + jnp.exp2(x * (_GELU_K1 + _GELU_K2 * (x * x))))


def _layer_spec(arr, layer):
    if layer is None:
        zeros = (0,) * arr.ndim
        return pl.BlockSpec(arr.shape, lambda *_: zeros, pipeline_mode=pl.Buffered(1))
    tail = (0,) * (arr.ndim - 1)
    return pl.BlockSpec((None,) + arr.shape[1:], lambda *_: (layer,) + tail,
                        pipeline_mode=pl.Buffered(1))


class _Casts:
    def __init__(self, items, ntiles, steps_per_tile):
        self.items = items
        self.ntiles = ntiles
        self.spt = steps_per_tile

    def __len__(self):
        return len(self.items)

    def args(self):
        return [arr for arr, _ in self.items]

    def in_specs(self):
        spt = self.spt
        return [pl.BlockSpec((None, arr.shape[1] // self.ntiles, arr.shape[2]),
                             functools.partial(lambda i, layer: (layer, i // spt, 0), layer=layer))
                for arr, layer in self.items]

    def out_specs(self):
        spt = self.spt
        return [pl.BlockSpec((arr.shape[1] // self.ntiles, arr.shape[2]), lambda i: (i // spt, 0))
                for arr, _ in self.items]

    def out_shapes(self):
        return [jax.ShapeDtypeStruct(arr.shape[1:], _BF16) for arr, _ in self.items]


def _run_casts(src_refs, dst_refs, steps_per_tile):
    def cast():
        for src, dst in zip(src_refs, dst_refs):
            dst[...] = src[...].astype(_BF16)

    if not src_refs:
        return
    if steps_per_tile == 1:
        cast()
    else:
        pl.when(pl.program_id(0) % steps_per_tile == 0)(cast)


def _mod_kernel(cond_ref, w_ref, b_ref, o_ref):
    c = cond_ref[...]
    s = (c * _sigmoid(c)).astype(_BF16)
    o_ref[0, 0] = _dot(s, w_ref[0].astype(_BF16)) + b_ref[0, 0]


def _modulation(cond8, mod_w, mod_b):
    d = D_MODEL
    return pl.pallas_call(
        _mod_kernel,
        grid=(DEPTH, N_MOD),
        in_specs=[
            pl.BlockSpec((MOD_ROWS, d), lambda l, n: (0, 0)),
            pl.BlockSpec((1, d, d), lambda l, n: (l, 0, n)),
            pl.BlockSpec((1, 1, 1, d), lambda l, n: (l, n, 0, 0)),
        ],
        out_specs=pl.BlockSpec((1, 1, MOD_ROWS, d), lambda l, n: (l, n, 0, 0)),
        out_shape=jax.ShapeDtypeStruct((DEPTH, N_MOD, MOD_ROWS, d), _F32),
        compiler_params=pltpu.CompilerParams(
            dimension_semantics=("arbitrary", "arbitrary"), vmem_limit_bytes=VMEM_LIMIT),
        name="modulation",
    )(cond8, mod_w, mod_b.reshape(DEPTH, N_MOD, 1, d))


def _ab_kernel(*refs, m, t, has_pos, has_h0, emit_state, n_cast):
    nseq = m // t
    lc = t // SUBLANES
    nplane = D_MODEL // LANES
    rc = min(ROW_CHUNK, t)
    conv_rows = CONF_ROWS // nseq
    refs = list(refs)
    x_ref = refs.pop(0)
    pos_ref = refs.pop(0) if has_pos else None
    perm_ref = refs.pop(0)
    mod_ref = refs.pop(0)
    ng_ref = refs.pop(0)
    h0_ref = refs.pop(0) if has_h0 else None
    (w_in_ref, cw_ref, cb_ref, wg_ref, bg_ref, lam_ref,
     ccw_ref, ccb_ref, lng_ref, lnb_ref, w_out_ref) = refs[:11]
    cast_src = refs[11:11 + n_cast]
    refs = refs[11 + n_cast:]
    xo_ref = refs.pop(0)
    st_ref = refs.pop(0) if emit_state else None
    cast_dst = refs[:n_cast]
    (h_scr, ycat_scr, plane_scr, af_scr, bf_scr, ab_scr, bb_scr, zg_scr, za_scr, zb_scr,
     xext_scr, gext_scr, wtap_scr) = refs[n_cast:]

    _run_casts(cast_src, cast_dst, 1)

    def residual(rows, cols=slice(None)):
        if has_pos:
            return x_ref[rows, cols] + pos_ref[rows, cols]
        return x_ref[rows, cols]

    mod = mod_ref[...]
    sh1, sc1, g1 = mod[0:1], mod[1:2], mod[2:3]

    for r0 in range(0, m, rc):
        hn = _rms(residual(slice(r0, r0 + rc)), ng_ref[0:1]) * (1.0 + sc1) + sh1
        ycat_scr[r0:r0 + rc, :D_MODEL] = hn.astype(_BF16)
    for r0 in range(0, m, rc):
        q, rq = divmod(r0, t)
        h_scr[r0:r0 + rc, :] = _dot(perm_ref[rq:rq + rc, :],
                                    ycat_scr[q * t:(q + 1) * t, :D_MODEL]).astype(_BF16)

    neg_lam = -lam_ref[...]
    softplus = jnp.maximum(neg_lam, 0.0) + jnp.log1p(jnp.exp(-jnp.abs(neg_lam)))
    decay = softplus * (-LRU_C * _LOG2E)

    sub = lax.broadcasted_iota(jnp.int32, (SUBLANES, COL_GROUP), 0)

    def front_halo(v):
        s_idx = lax.broadcasted_iota(jnp.int32, v.shape, 0) & (SUBLANES - 1)
        return jnp.where(s_idx == 0, 0.0, pltpu.roll(v, 1, 0))

    def back_halo(v):
        s_idx = lax.broadcasted_iota(jnp.int32, v.shape, 0) & (SUBLANES - 1)
        return jnp.where(s_idx == SUBLANES - 1, 0.0, pltpu.roll(v, v.shape[0] - 1, 0))

    def fill_halos(ext_scr, halo):
        hr = halo * SUBLANES
        for q in range(nseq):
            ext_scr[q, 0:hr, :] = front_halo(ext_scr[q, t:t + hr, :])
            ext_scr[q, hr + t:, :] = back_halo(ext_scr[q, hr:2 * hr, :])

    def project(dst_ref, row_slice, r0, col0):
        dst_ref[row_slice] = _dot(h_scr[r0:r0 + rc, :], w_in_ref[:, col0:col0 + COL_GROUP])

    def project_x(grp, r0):
        q, rq = divmod(r0, t)
        hr = LRU_LEFT * SUBLANES
        project(xext_scr, (q, slice(hr + rq, hr + rq + rc), slice(None)), r0,
                LRU_WIDTH + grp * COL_GROUP)

    for r0 in range(0, m, rc):
        project_x(0, r0)
    fill_halos(xext_scr, LRU_LEFT)

    for grp in range(N_GROUPS):
        c0 = grp * COL_GROUP
        cols = slice(c0, c0 + COL_GROUP)

        heads_per_group = COL_GROUP // LRU_BLOCK
        for r0 in range(0, m, rc):
            q, rq = divmod(r0, t)
            project(zg_scr, slice(r0, r0 + rc), r0, c0)
            project(za_scr, slice(r0, r0 + rc), r0, 2 * LRU_WIDTH + c0)
            project(zb_scr, slice(r0, r0 + rc), r0, 2 * LRU_WIDTH + CONF_WIDTH + c0)
            xc = jnp.broadcast_to(cb_ref[:, cols], (rc, COL_GROUP))
            for k in range(LRU_CONV):
                off = rq + k * SUBLANES
                xc = xc + cw_ref[k:k + 1, cols] * xext_scr[q, off:off + rc, :]
            xcb = xc.astype(_BF16)
            gl_parts = [
                _dot(xcb[:, hh * LRU_BLOCK:(hh + 1) * LRU_BLOCK],
                     wg_ref[grp * heads_per_group + hh])
                for hh in range(heads_per_group)]

            def gate(dg):
                parts = [gp[:, dg * LRU_BLOCK:(dg + 1) * LRU_BLOCK] for gp in gl_parts]
                return _sigmoid(jnp.concatenate(parts, axis=1) + bg_ref[dg:dg + 1, cols])

            for d, (a_scr, b_scr) in enumerate(((af_scr, bf_scr), (ab_scr, bb_scr))):
                r = gate(2 * d)
                i_gate = gate(2 * d + 1)
                a = jnp.exp2(r * decay[d:d + 1, cols])
                y = 1.0 - a * a
                a_scr[r0:r0 + rc, :] = a
                b_scr[r0:r0 + rc, :] = jnp.where(y > 0.0, y * lax.rsqrt(y), 0.0) * i_gate * xc

        def scan_body(i, carry):
            out = []
            for q in range(nseq):
                hf, df, hb, db = carry[q]
                rows_f = pl.ds(pl.multiple_of(q * t + i * SUBLANES, SUBLANES), SUBLANES)
                rows_b = pl.ds(pl.multiple_of(q * t + (lc - 1 - i) * SUBLANES, SUBLANES), SUBLANES)
                a = af_scr[rows_f, :]
                hf = a * hf + bf_scr[rows_f, :]
                df = df * a
                bf_scr[rows_f, :] = hf
                af_scr[rows_f, :] = df
                a = ab_scr[rows_b, :]
                hb = a * hb + bb_scr[rows_b, :]
                db = db * a
                bb_scr[rows_b, :] = hb
                ab_scr[rows_b, :] = db
                out.append((hf, df, hb, db))
            return tuple(out)

        zeros8 = jnp.zeros((SUBLANES, COL_GROUP), _F32)
        ones8 = jnp.ones((SUBLANES, COL_GROUP), _F32)
        ends = lax.fori_loop(0, lc, scan_body,
                             tuple((zeros8, ones8, zeros8, ones8) for _ in range(nseq)),
                             unroll=SCAN_UNROLL)

        entry = []
        for q in range(nseq):
            hf_end, df_end, hb_end, db_end = ends[q]
            if has_h0:
                cf = h0_ref[q, 0:1, cols]
                cb = h0_ref[q, 1:2, cols]
            else:
                cf = jnp.zeros((1, COL_GROUP), _F32)
                cb = cf
            cf_all = jnp.zeros((SUBLANES, COL_GROUP), _F32)
            for s in range(SUBLANES):
                cf_all = jnp.where(sub == s, cf, cf_all)
                cf = hf_end[s:s + 1] + df_end[s:s + 1] * cf
            cb_all = jnp.zeros((SUBLANES, COL_GROUP), _F32)
            for s in reversed(range(SUBLANES)):
                cb_all = jnp.where(sub == s, cb, cb_all)
                cb = hb_end[s:s + 1] + db_end[s:s + 1] * cb
            if emit_state:
                st_ref[q, 0:1, cols] = cf
                st_ref[q, 1:2, cols] = cb
            entry.append((cf_all, cb_all))
        gr = CONF_HALF * SUBLANES
        for r0 in range(0, m, rc):
            q, rq = divmod(r0, t)
            rows = slice(r0, r0 + rc)
            if grp + 1 < N_GROUPS:
                project_x(grp + 1, r0)
            cf_all, cb_all = entry[q]
            shape3 = (rc // SUBLANES, SUBLANES, COL_GROUP)
            hsum = (bf_scr[rows, :].reshape(shape3) + af_scr[rows, :].reshape(shape3) * cf_all[None]
                    + bb_scr[rows, :].reshape(shape3) + ab_scr[rows, :].reshape(shape3) * cb_all[None])
            ycat_scr[rows, cols] = (
                _gelu(zg_scr[rows, :]) * hsum.reshape(rc, COL_GROUP)).astype(_BF16)
            gext_scr[q, gr + rq:gr + rq + rc, :] = za_scr[rows, :] * _sigmoid(zb_scr[rows, :])
        fill_halos(gext_scr, CONF_HALF)
        if grp + 1 < N_GROUPS:
            fill_halos(xext_scr, LRU_LEFT)

        for k in range(CONF_KERNEL):
            wtap_scr[k] = jnp.broadcast_to(ccw_ref[k:k + 1, cols], (SUBLANES, COL_GROUP))
        shape3 = (conv_rows // SUBLANES, SUBLANES, COL_GROUP)

        def conv_body(i, _):
            r0 = pl.multiple_of(i * conv_rows, conv_rows)
            for q in range(nseq):
                acc = jnp.broadcast_to(ccb_ref[:, cols], shape3)
                for k in range(CONF_KERNEL):
                    tile = gext_scr[q, pl.ds(r0 + k * SUBLANES, conv_rows), :]
                    acc = acc + wtap_scr[k][None] * tile.reshape(shape3)
                acc = acc.reshape(conv_rows, COL_GROUP)
                for v in range(COL_GROUP // LANES):
                    plane_scr[grp * (COL_GROUP // LANES) + v, pl.ds(q * t + r0, conv_rows), :] = (
                        acc[:, v * LANES:(v + 1) * LANES])
            return 0

        lax.fori_loop(0, t // conv_rows, conv_body, 0, unroll=2)

    for r0 in range(0, m, rc):
        rows = slice(r0, r0 + rc)
        gc = jnp.concatenate([plane_scr[p, rows, :] for p in range(nplane)], axis=1)
        mu = jnp.mean(gc, axis=-1, keepdims=True)
        gcc = gc - mu
        var = jnp.mean(gcc * gcc, axis=-1, keepdims=True)
        yb = gcc * lax.rsqrt(var + EPS) * lng_ref[...] + lnb_ref[...]
        ycat_scr[rows, LRU_WIDTH:] = (yb * _sigmoid(yb)).astype(_BF16)
        upd = g1 * _rms(_dot(ycat_scr[rows, :], w_out_ref[...]), ng_ref[1:2])
        for p in range(nplane):
            plane_scr[p, rows, :] = upd[:, p * LANES:(p + 1) * LANES]

    def unpermute_body(i, _):
        for q in range(nseq):
            for s in range(SUBLANES):
                dst = pl.ds(pl.multiple_of(q * t + s * lc + i * SUBLANES, SUBLANES), SUBLANES)
                for p in range(nplane):
                    u = plane_scr[p, pl.ds(q * t + i * SUBLANES * SUBLANES + s, SUBLANES,
                                          stride=SUBLANES), :]
                    pc = slice(p * LANES, (p + 1) * LANES)
                    xo_ref[dst, pc] = residual(dst, pc) + u
        return 0

    lax.fori_loop(0, lc // SUBLANES, unpermute_body, 0, unroll=LOOP_UNROLL)


def _mixer_ab(x_all, pos, mod_tbl, norm_g, h0_all, w_in, cw, cb, wg, bg, lam, ccw, ccb, lng, lnb,
              w_out, *, layer, m, t, row0, nblk, mod_row0, mod_per_block, emit_state, casts=None):
    rows, d = x_all.shape
    nseq = m // t
    blk0 = row0 // m
    j = layer // 2
    has_pos = pos is not None
    has_h0 = h0_all is not None
    n_cast = len(casts) if casts is not None else 0
    single = dict(pipeline_mode=pl.Buffered(1))
    big = m * d * 4 > 2 * 1024 * 1024
    args = [x_all]
    in_specs = [pl.BlockSpec((m, d), lambda i: (i + blk0, 0), **(single if big and has_pos else {}))]
    if has_pos:
        args.append(pos)
        in_specs.append(pl.BlockSpec(pos.shape, lambda i: (0, 0), **single))
    idx = np.arange(t)
    perm = np.zeros((t, t), np.float32)
    perm[idx, (idx % SUBLANES) * (t // SUBLANES) + idx // SUBLANES] = 1.0
    args.append(jnp.asarray(perm, _BF16))
    in_specs.append(pl.BlockSpec((t, t), lambda i: (0, 0), **single))
    args.append(mod_tbl)
    in_specs.append(pl.BlockSpec(
        (None, None, N_MOD, d),
        (lambda i: (layer, i + mod_row0, 0, 0)) if mod_per_block
        else (lambda i: (layer, mod_row0, 0, 0))))
    args.append(norm_g)
    in_specs.append(_layer_spec(norm_g, layer))
    if has_h0:
        args.append(h0_all)
        in_specs.append(pl.BlockSpec((nseq, None, 2, LRU_WIDTH), lambda i: (i, j, 0, 0)))
    for w in (w_in, cw, cb, wg, bg, lam, ccw, ccb, lng, lnb, w_out):
        args.append(w)
        in_specs.append(_layer_spec(w, None if w.ndim == 2 else j))
    out_shape = [jax.ShapeDtypeStruct((rows, d), _F32)]
    out_specs = [pl.BlockSpec((m, d), lambda i: (i + blk0, 0), **(single if big else {}))]
    if emit_state:
        out_shape.append(jax.ShapeDtypeStruct((nblk * nseq, 2, LRU_WIDTH), _F32))
        out_specs.append(pl.BlockSpec((nseq, 2, LRU_WIDTH), lambda i: (i, 0, 0)))
    if n_cast:
        args += casts.args()
        in_specs += casts.in_specs()
        out_shape += casts.out_shapes()
        out_specs += casts.out_specs()
    scratch = [
        pltpu.VMEM((m, d), _BF16),
        pltpu.VMEM((m, LRU_WIDTH + CONF_WIDTH), _BF16),
        pltpu.VMEM((d // LANES, m, LANES), _F32),
    ] + [pltpu.VMEM((m, COL_GROUP), _F32)] * 7 + [
        pltpu.VMEM((nseq, t + 2 * LRU_LEFT * SUBLANES, COL_GROUP), _F32),
        pltpu.VMEM((nseq, t + 2 * CONF_HALF * SUBLANES, COL_GROUP), _F32),
        pltpu.VMEM((CONF_KERNEL, SUBLANES, COL_GROUP), _F32),
    ]
    res = pl.pallas_call(
        functools.partial(_ab_kernel, m=m, t=t, has_pos=has_pos, has_h0=has_h0,
                          emit_state=emit_state, n_cast=n_cast),
        grid=(nblk,),
        in_specs=in_specs,
        out_specs=out_specs,
        out_shape=out_shape,
        scratch_shapes=scratch,
        input_output_aliases={0: 0},
        compiler_params=pltpu.CompilerParams(
            dimension_semantics=("arbitrary",), vmem_limit_bytes=VMEM_LIMIT),
        name="mixer_ab",
    )(*args)
    n_main = 2 if emit_state else 1
    return res[0], (res[1] if emit_state else None), list(res[n_main:])


def _c_kernel(*refs, m, n_cast, steps_per_tile):
    (x_ref, mod_ref, ng_ref, w_in_ref, lng_ref, lnb_ref, ws_ref, bs_ref, w_out_ref) = refs[:9]
    cast_src = refs[9:9 + n_cast]
    xo_ref = refs[9 + n_cast]
    cast_dst = refs[10 + n_cast:10 + 2 * n_cast]
    h_scr, v_scr, y_scr, u_scr = refs[10 + 2 * n_cast:]
    rc = min(ROW_CHUNK, m)

    def gate_branch(hd):
        return _gelu(_dot(h_scr[...], w_in_ref[:, hd * SGU_HEAD_DIM:(hd + 1) * SGU_HEAD_DIM]))

    _run_casts(cast_src, cast_dst, steps_per_tile)
    mod = mod_ref[...]
    sh1, sc1, g1 = mod[0:1], mod[1:2], mod[2:3]
    for r0 in range(0, m, rc):
        h_scr[r0:r0 + rc, :] = (
            _rms(x_ref[r0:r0 + rc, :], ng_ref[0:1]) * (1.0 + sc1) + sh1).astype(_BF16)
    h = h_scr[...]

    for hd in range(SGU_HEADS):
        c0 = SGU_WIDTH + hd * SGU_HEAD_DIM
        v_scr[:, hd * SGU_HEAD_DIM:(hd + 1) * SGU_HEAD_DIM] = _gelu(
            _dot(h, w_in_ref[:, c0:c0 + SGU_HEAD_DIM]))
    for hd in range(C_EARLY_HEADS):
        u_scr[:, hd * SGU_HEAD_DIM:(hd + 1) * SGU_HEAD_DIM] = gate_branch(hd)
    v = v_scr[...]
    mu = jnp.mean(v, axis=-1, keepdims=True)
    vc = v - mu
    var = jnp.mean(vc * vc, axis=-1, keepdims=True)
    rstd = lax.rsqrt(var + EPS)

    for hd in range(SGU_HEADS):
        cols = slice(hd * SGU_HEAD_DIM, (hd + 1) * SGU_HEAD_DIM)
        vn = ((v_scr[:, cols] - mu) * rstd * lng_ref[:, cols] + lnb_ref[:, cols]).astype(_BF16)
        u = u_scr[:, cols] if hd < C_EARLY_HEADS else gate_branch(hd)
        w_s = ws_ref[hd]
        bias = bs_ref[:, hd:hd + 1]
        for n in range(m // CHUNK):
            rows = slice(n * CHUNK, (n + 1) * CHUNK)
            s = _dot(w_s, vn[rows]) + bias
            y_scr[rows, cols] = (u[rows] * s).astype(_BF16)

    for r0 in range(0, m, rc):
        rows = slice(r0, r0 + rc)
        o = _dot(y_scr[rows, :], w_out_ref[...])
        xo_ref[rows, :] = x_ref[rows, :] + g1 * _rms(o, ng_ref[1:2])


def _mixer_c(x_all, mod_tbl, norm_g, w_in, lng, lnb, ws, bs_t, w_out, *, layer, m, mod_index,
             casts=None):
    rows, d = x_all.shape
    j = layer // 2
    n_cast = len(casts) if casts is not None else 0
    args = [x_all, mod_tbl, norm_g, w_in, lng, lnb, ws, bs_t, w_out]
    in_specs = [
        pl.BlockSpec((m, d), lambda i: (i, 0)),
        pl.BlockSpec((None, None, N_MOD, d), lambda i: (layer, mod_index(i), 0, 0)),
        _layer_spec(norm_g, layer),
    ] + [_layer_spec(w, None if w.ndim == 2 else j) for w in (w_in, lng, lnb, ws, bs_t, w_out)]
    out_specs = [pl.BlockSpec((m, d), lambda i: (i, 0))]
    out_shape = [jax.ShapeDtypeStruct((rows, d), _F32)]
    if n_cast:
        args += casts.args()
        in_specs += casts.in_specs()
        out_shape += casts.out_shapes()
        out_specs += casts.out_specs()
    res = pl.pallas_call(
        functools.partial(_c_kernel, m=m, n_cast=n_cast,
                          steps_per_tile=casts.spt if n_cast else 1),
        grid=(rows // m,),
        in_specs=in_specs,
        out_specs=out_specs,
        out_shape=out_shape,
        scratch_shapes=[
            pltpu.VMEM((m, d), _BF16),
            pltpu.VMEM((m, SGU_WIDTH), _F32),
            pltpu.VMEM((m, SGU_WIDTH), _BF16),
            pltpu.VMEM((m, C_EARLY_HEADS * SGU_HEAD_DIM), _F32),
        ],
        compiler_params=pltpu.CompilerParams(
            dimension_semantics=("arbitrary",), vmem_limit_bytes=VMEM_LIMIT),
        name="mixer_c",
    )(*args)
    return res[0], list(res[1:])


def _ffn_kernel(x_ref, mod_ref, ng_ref, wup_ref, cw_ref, cb_ref, wd_ref, xo_ref, h_scr, act_scr,
                *, m, seg, n_short_blocks, t_long, blk0):
    long_seq = pl.program_id(0) + blk0 >= n_short_blocks
    mod = mod_ref[...]
    sh2, sc2, g2 = mod[3:4], mod[4:5], mod[5:6]
    for r0 in range(0, m, ROW_CHUNK):
        h_scr[r0:r0 + ROW_CHUNK, :] = (
            _rms(x_ref[r0:r0 + ROW_CHUNK, :], ng_ref[2:3]) * (1.0 + sc2) + sh2).astype(_BF16)
    h = h_scr[...]
    tn = COL_GROUP
    sub = lax.broadcasted_iota(jnp.int32, (SUBLANES, tn), 0)

    def conv(z, c0):
        outs = []
        for g in range(m // seg):
            zs = z[g * seg:(g + 1) * seg]
            prev = pltpu.roll(zs, 1, 0)
            nxt = pltpu.roll(zs, seg - 1, 0)
            if g > 0 and t_long > seg:
                before = jnp.where(long_seq, z[g * seg - 1:g * seg], 0.0)
            else:
                before = jnp.zeros((1, tn), _F32)
            if g < m // seg - 1 and t_long > seg:
                after = jnp.where(long_seq, z[(g + 1) * seg:(g + 1) * seg + 1], 0.0)
            else:
                after = jnp.zeros((1, tn), _F32)
            prev = jnp.concatenate(
                [jnp.where(sub == 0, before, prev[:SUBLANES]), prev[SUBLANES:]], axis=0)
            nxt = jnp.concatenate(
                [nxt[:seg - SUBLANES],
                 jnp.where(sub == SUBLANES - 1, after, nxt[seg - SUBLANES:])], axis=0)
            cols = slice(c0, c0 + tn)
            outs.append(cw_ref[0:1, cols] * prev + cw_ref[1:2, cols] * zs
                        + cw_ref[2:3, cols] * nxt + cb_ref[:, cols])
        return jnp.concatenate(outs, axis=0)

    for j in range(D_FF // tn):
        gate = conv(_dot(h, wup_ref[:, j * tn:(j + 1) * tn]), j * tn)
        val = conv(_dot(h, wup_ref[:, D_FF + j * tn:D_FF + (j + 1) * tn]), D_FF + j * tn)
        act_scr[:, j * tn:(j + 1) * tn] = (_gelu(gate) * val).astype(_BF16)

    for r0 in range(0, m, ROW_CHUNK):
        rows = slice(r0, r0 + ROW_CHUNK)
        o = _dot(act_scr[rows, :], wd_ref[...])
        xo_ref[rows, :] = x_ref[rows, :] + g2 * _rms(o, ng_ref[3:4])


def _conv_ffn(x_all, mod_tbl, norm_g, w_up, conv_w, conv_b, w_down, *, layer, m, seg,
              n_short_blocks, t_long, mod_index, blk0=0, nblk=None):
    rows, d = x_all.shape
    if nblk is None:
        nblk = rows // m
    return pl.pallas_call(
        functools.partial(_ffn_kernel, m=m, seg=seg, n_short_blocks=n_short_blocks,
                          t_long=t_long, blk0=blk0),
        grid=(nblk,),
        in_specs=[
            pl.BlockSpec((m, d), lambda i: (i + blk0, 0)),
            pl.BlockSpec((None, None, N_MOD, d), lambda i: (layer, mod_index(i + blk0), 0, 0)),
        ] + [_layer_spec(w, None if w.ndim == 2 else layer)
             for w in (norm_g, w_up, conv_w, conv_b, w_down)],
        out_specs=pl.BlockSpec((m, d), lambda i: (i, 0)),
        out_shape=jax.ShapeDtypeStruct((nblk * m, d), _F32),
        scratch_shapes=[pltpu.VMEM((m, d), _BF16), pltpu.VMEM((m, D_FF), _BF16)],
        compiler_params=pltpu.CompilerParams(
            dimension_semantics=("arbitrary",), vmem_limit_bytes=VMEM_LIMIT),
        name="conv_ffn",
    )(x_all, mod_tbl, norm_g, w_up, conv_w, conv_b, w_down)


def _grid_pos_embedding(t, d):
    rows = t // GRID_W
    row = jnp.repeat(jnp.arange(rows), GRID_W).astype(_F32)
    col = jnp.tile(jnp.arange(GRID_W), rows).astype(_F32)
    quarter = d // 4
    freq = jnp.exp(-math.log(POS_BASE) * jnp.arange(quarter, dtype=_F32) / quarter)

    def enc(p):
        ang = p[:, None] * freq[None, :]
        return jnp.concatenate([jnp.sin(ang), jnp.cos(ang)], axis=-1)

    return jnp.concatenate([enc(row), enc(col)], axis=-1)


def kernel(x_prompt, x_sample, state_lru, c, c_ctx, mod_w, mod_b, norm_g, ab_w_in, lru_conv_w, lru_conv_b, lru_w_gates, lru_b_gates, lru_lambda, conf_conv_w, conf_conv_b, conf_ln_g, conf_ln_b, ab_w_out, c_w_in, c_ln_g, c_ln_b, c_w_s, c_b_s, c_w_out, ffn_w_up, ffn_conv_w, ffn_conv_b, ffn_w_down):
    d = D_MODEL
    batch, seq, _ = x_prompt.shape
    dec_batch, dec_seq, _ = x_sample.shape
    n_ab = ab_w_in.shape[0]
    rows_p = batch * seq
    rows_s = dec_batch * dec_seq

    cond8 = jnp.concatenate(
        [c_ctx[None, :], c, jnp.zeros((MOD_ROWS - 1 - dec_batch, d), _F32)], axis=0)
    mod_tbl = jnp.transpose(_modulation(cond8, mod_w, mod_b), (0, 2, 1, 3))

    def mod_index(block_rows):
        first_latent = rows_p // block_rows
        per_seq = dec_seq // block_rows
        return lambda i: jnp.maximum((i - first_latent) // per_seq + 1, 0)

    wg_b = jnp.transpose(lru_w_gates, (0, 3, 4, 1, 2, 5)).reshape(
        n_ab, LRU_HEADS, LRU_BLOCK, 4 * LRU_BLOCK).astype(_BF16)
    p_ab = (lru_conv_w, lru_conv_b[:, None, :], wg_b, lru_b_gates.reshape(n_ab, 4, LRU_WIDTH),
            lru_lambda, conf_conv_w, conf_conv_b[:, None, :], conf_ln_g[:, None, :],
            conf_ln_b[:, None, :])
    p_c = (c_ln_g[:, None, :], c_ln_b[:, None, :], c_w_s.astype(_BF16),
           jnp.transpose(c_b_s, (0, 2, 1)))

    pos = _grid_pos_embedding(dec_seq, d)
    x_all = jnp.concatenate([x_prompt.reshape(rows_p, d), x_sample.reshape(rows_s, d)], axis=0)
    m_p = 2 * seq
    m_s = dec_seq
    ab_blocks = rows_p // m_p
    c_blocks = (rows_p + rows_s) // C_ROWS
    c_tiles = 8

    mix_in, mix_out = ab_w_in[0].astype(_BF16), ab_w_out[0].astype(_BF16)
    states = []
    for l in range(DEPTH):
        j = l // 2
        items = [(ffn_w_up, l), (ffn_w_down, l)]
        if l % 2 == 0:
            items += [(c_w_in, j), (c_w_out, j)]
            x_all, st, cast = _mixer_ab(
                x_all, None, mod_tbl, norm_g, None, mix_in, *p_ab, mix_out, layer=l, m=m_p, t=seq,
                row0=0, nblk=ab_blocks, mod_row0=0, mod_per_block=False, emit_state=True,
                casts=_Casts(items, ab_blocks, 1))
            states.append(st)
            x_all, _, _ = _mixer_ab(
                x_all, pos if l == 0 else None, mod_tbl, norm_g, state_lru, mix_in, *p_ab, mix_out,
                layer=l, m=m_s, t=dec_seq, row0=rows_p, nblk=dec_batch, mod_row0=1,
                mod_per_block=True, emit_state=False)
        else:
            if l + 1 < DEPTH:
                items += [(ab_w_in, j + 1), (ab_w_out, j + 1)]
            x_all, cast = _mixer_c(
                x_all, mod_tbl, norm_g, mix_in, *p_c, mix_out, layer=l, m=C_ROWS,
                mod_index=mod_index(C_ROWS), casts=_Casts(items, c_tiles, c_blocks // c_tiles))
        w_up_b, w_down_b = cast[0], cast[1]
        if len(cast) > 2:
            mix_in, mix_out = cast[2], cast[3]
        ffn = functools.partial(
            _conv_ffn, x_all, mod_tbl, norm_g, w_up_b, ffn_conv_w, ffn_conv_b[:, None, :],
            w_down_b, layer=l, m=FFN_ROWS, seg=seq, n_short_blocks=rows_p // FFN_ROWS,
            t_long=dec_seq, mod_index=mod_index(FFN_ROWS))
        if l + 1 < DEPTH:
            x_all = ffn()
        else:
            y_p = ffn(blk0=0, nblk=rows_p // FFN_ROWS)
            y_s = ffn(blk0=rows_p // FFN_ROWS, nblk=rows_s // FFN_ROWS)

    y_prompt = y_p.reshape(batch, seq, d)
    y_sample = y_s.reshape(dec_batch, dec_seq, d)
    new_state = jnp.stack(states, axis=1)
    return (y_prompt, y_sample, new_state)
```
